```python
import math
import jax, jax.numpy as jnp
from jax import lax
import numpy as np

D_MODEL = 2048
BATCH = 2
SEQ = 4096
DEPTH = 4
DEC_BATCH = 128
DEC_SEQ = 1
PAST_LEN = 8192
PAGE_SIZE = 128

N_MIXERS = 3
N_A = (DEPTH + 2) // 3
N_B = (DEPTH + 1) // 3
N_C = DEPTH // 3

ROPE_THETA = 10000.0
NORM_EPS = 1e-6
Q_BLOCK = 128

FFN_DIM = 5632

DIFF_HEAD_DIM = 128
DIFF_HEADS = D_MODEL // (2 * DIFF_HEAD_DIM)
DIFF_KV_HEADS = 2
DIFF_GROUP = DIFF_HEADS // DIFF_KV_HEADS
DIFF_QKV_DIM = (DIFF_HEADS + 2 * DIFF_KV_HEADS) * 2 * DIFF_HEAD_DIM

SWA_HEAD_DIM = 64
SWA_HEADS = D_MODEL // SWA_HEAD_DIM
SWA_KV_HEADS = SWA_HEADS // 8
SWA_GROUP = SWA_HEADS // SWA_KV_HEADS
SWA_QKV_DIM = (SWA_HEADS + 2 * SWA_KV_HEADS) * SWA_HEAD_DIM
WINDOW = 128

MLA_HEADS = 16
MLA_NOPE = 128
MLA_ROPE = 64
MLA_V = 128
MLA_Q_LORA = 512
MLA_KV_LORA = 512
MLA_DOWN_DIM = MLA_Q_LORA + MLA_KV_LORA + MLA_ROPE

kernel_name = 'hybrid_diff_swa_mla_decoder_step'


def rms_norm(x, gain):
    xf = x.astype(jnp.float32)
    y = xf * lax.rsqrt(jnp.mean(xf * xf, axis=-1, keepdims=True) + NORM_EPS)
    return (y * gain.astype(jnp.float32)).astype(x.dtype)


def rope(x, pos):
    half = x.shape[-1] // 2
    inv_freq = jnp.power(ROPE_THETA, -jnp.arange(half, dtype=jnp.float32) / half)
    ang = pos.astype(jnp.float32)[:, None] * inv_freq[None, :]
    shape = (pos.shape[0],) + (1,) * (x.ndim - 3) + (half,)
    cos = jnp.cos(ang).reshape(shape).astype(x.dtype)
    sin = jnp.sin(ang).reshape(shape).astype(x.dtype)
    x1, x2 = x[..., :half], x[..., half:]
    return jnp.concatenate([x1 * cos - x2 * sin, x2 * cos + x1 * sin], axis=-1)


def swiglu(h, w_gate, w_up, w_down):
    return (jax.nn.silu(h @ w_gate) * (h @ w_up)) @ w_down


def masked_softmax(s, mask):
    return jax.nn.softmax(jnp.where(mask, s, -jnp.inf), axis=-1)


def over_query_blocks(fn, qs, q_pos):
    sq = q_pos.shape[0]
    qb = math.gcd(sq, Q_BLOCK)
    nb = sq // qb
    def split(a):
        return jnp.moveaxis(a.reshape(a.shape[0], nb, qb, *a.shape[2:]), 1, 0)
    out = lax.map(lambda args: fn(*args[0], args[1]), (tuple(split(a) for a in qs), q_pos.reshape(nb, qb)))
    out = jnp.moveaxis(out, 0, 1)
    return out.reshape(out.shape[0], sq, *out.shape[3:])


def gather_pages(cache, page_table):
    g = cache[page_table]
    return g.reshape(g.shape[0], g.shape[1] * g.shape[2], *g.shape[3:])


def diff_attention(h, pos, past_k, past_v, w_qkv, q_gain, k_gain, lq1, lk1, lq2, lk2, sub_gain, w_o, lam_init):
    b, s, _ = h.shape
    nq = DIFF_HEADS * 2 * DIFF_HEAD_DIM
    nk = DIFF_KV_HEADS * 2 * DIFF_HEAD_DIM
    qkv = h @ w_qkv
    q = qkv[..., :nq].reshape(b, s, DIFF_HEADS, 2, DIFF_HEAD_DIM)
    k = qkv[..., nq:nq + nk].reshape(b, s, DIFF_KV_HEADS, 2, DIFF_HEAD_DIM)
    v = qkv[..., nq + nk:].reshape(b, s, DIFF_KV_HEADS, 2 * DIFF_HEAD_DIM)
    q = rope(rms_norm(q, q_gain), pos)
    k = rope(rms_norm(k, k_gain), pos)
    if past_k is None:
        k_all, v_all = k, v
    else:
        pk = past_k.reshape(b, past_k.shape[1], DIFF_KV_HEADS, 2, DIFF_HEAD_DIM)
        k_all = jnp.concatenate([pk, k], axis=1)
        v_all = jnp.concatenate([past_v, v], axis=1)
    k_pos = jnp.arange(k_all.shape[1])
    f32 = jnp.float32
    lam = (jnp.exp(jnp.sum(lq1.astype(f32) * lk1.astype(f32)))
           - jnp.exp(jnp.sum(lq2.astype(f32) * lk2.astype(f32))) + lam_init)
    scale = DIFF_HEAD_DIM ** -0.5

    def block(q_blk, qp):
        sc = jnp.einsum('bqhgcd,bkhcd->bhgcqk', q_blk, k_all).astype(f32) * scale
        p = masked_softmax(sc, qp[:, None] >= k_pos[None, :])
        a = (p[:, :, :, 0] - lam * p[:, :, :, 1]).astype(v_all.dtype)
        return jnp.einsum('bhgqk,bkhe->bqhge', a, v_all)

    qg = q.reshape(b, s, DIFF_KV_HEADS, DIFF_GROUP, 2, DIFF_HEAD_DIM)
    o = over_query_blocks(block, (qg,), pos)
    o = rms_norm(o.reshape(b, s, DIFF_HEADS, 2 * DIFF_HEAD_DIM), sub_gain) * (1.0 - lam_init)
    out = o.reshape(b, s, nq) @ w_o
    return out, k.reshape(b, s, DIFF_KV_HEADS, 2 * DIFF_HEAD_DIM), v


def sink_attention(q, k, v, mask, sink):
    sc = jnp.einsum('...qhgd,...khd->...hgqk', q, k).astype(jnp.float32) * (SWA_HEAD_DIM ** -0.5)
    sc = jnp.where(mask, sc, -jnp.inf)
    sink_col = jnp.broadcast_to(sink.astype(jnp.float32)[:, :, None, None], sc.shape[:-1] + (1,))
    p = jax.nn.softmax(jnp.concatenate([sc, sink_col], axis=-1), axis=-1)[..., :-1]
    return jnp.einsum('...hgqk,...khd->...qhgd', p.astype(v.dtype), v)


def sliding_window_attention(h, pos, past_k, past_v, w_qkv, q_gain, k_gain, sinks, w_o):
    b, s, _ = h.shape
    nq = SWA_HEADS * SWA_HEAD_DIM
    nk = SWA_KV_HEADS * SWA_HEAD_DIM
    qkv = h @ w_qkv
    q = qkv[..., :nq].reshape(b, s, SWA_KV_HEADS, SWA_GROUP, SWA_HEAD_DIM)
    k = qkv[..., nq:nq + nk].reshape(b, s, SWA_KV_HEADS, SWA_HEAD_DIM)
    v = qkv[..., nq + nk:].reshape(b, s, SWA_KV_HEADS, SWA_HEAD_DIM)
    q = rope(rms_norm(q, q_gain), pos)
    k = rope(rms_norm(k, k_gain), pos)
    sink = sinks.reshape(SWA_KV_HEADS, SWA_GROUP)
    if past_k is None:
        nb = s // WINDOW
        q_blk = q.reshape(b, nb, WINDOW, SWA_KV_HEADS, SWA_GROUP, SWA_HEAD_DIM)
        def band(a):
            a = a.reshape(b, nb, WINDOW, *a.shape[2:])
            prev = jnp.concatenate([jnp.zeros_like(a[:, :1]), a[:, :-1]], axis=1)
            return jnp.concatenate([prev, a], axis=2)
        k_blk, v_blk = band(k), band(v)
        q_pos = pos.reshape(nb, WINDOW)
        k_pos = (jnp.arange(nb)[:, None] - 1) * WINDOW + jnp.arange(2 * WINDOW)[None, :]
        dist = q_pos[:, :, None] - k_pos[:, None, :]
        mask = ((dist >= 0) & (dist <= WINDOW) & (k_pos[:, None, :] >= 0))[:, None, None]
        o = sink_attention(q_blk, k_blk, v_blk, mask, sink)
        w_keep = min(WINDOW, s)
        new_k, new_v = k[:, s - w_keep:], v[:, s - w_keep:]
    else:
        w_buf = past_k.shape[1]
        k_all = jnp.concatenate([past_k, k], axis=1)
        v_all = jnp.concatenate([past_v, v], axis=1)
        k_pos = pos[0] - w_buf + jnp.arange(k_all.shape[1])
        dist = pos[:, None] - k_pos[None, :]
        mask = (dist >= 0) & (dist <= WINDOW)
        o = sink_attention(q, k_all, v_all, mask, sink)
        new_k, new_v = k_all[:, -w_buf:], v_all[:, -w_buf:]
    out = o.reshape(b, s, nq) @ w_o
    return out, new_k, new_v


def latent_attention(h, pos, past_lat, past_kr, w_down, q_a_gain, kv_a_gain, w_uq, qn_gain, qr_gain,
                     kn_gain, kr_gain, w_uk, w_uv, w_o):
    b, s, _ = h.shape
    d = h @ w_down
    c_q = rms_norm(d[..., :MLA_Q_LORA], q_a_gain)
    c_kv = rms_norm(d[..., MLA_Q_LORA:MLA_Q_LORA + MLA_KV_LORA], kv_a_gain)
    k_rope = rope(rms_norm(d[..., MLA_Q_LORA + MLA_KV_LORA:], kr_gain), pos)
    q = (c_q @ w_uq).reshape(b, s, MLA_HEADS, MLA_NOPE + MLA_ROPE)
    q_nope = rms_norm(q[..., :MLA_NOPE], qn_gain)
    q_rope = rope(rms_norm(q[..., MLA_NOPE:], qr_gain), pos)
    if past_lat is None:
        lat_all, kr_all = c_kv, k_rope
    else:
        lat_all = jnp.concatenate([past_lat, c_kv], axis=1)
        kr_all = jnp.concatenate([past_kr, k_rope], axis=1)
    k_nope = rms_norm(jnp.einsum('bkc,chd->bkhd', lat_all, w_uk), kn_gain)
    k_pos = jnp.arange(lat_all.shape[1])
    scale = (MLA_NOPE + MLA_ROPE) ** -0.5

    def block(qn, qr, qp):
        sc = (jnp.einsum('bqhd,bkhd->bhqk', qn, k_nope)
              + jnp.einsum('bqhr,bkr->bhqk', qr, kr_all)).astype(jnp.float32) * scale
        p = masked_softmax(sc, qp[:, None] >= k_pos[None, :])
        return jnp.einsum('bhqk,bkc->bqhc', p.astype(lat_all.dtype), lat_all)

    o_lat = over_query_blocks(block, (q_nope, q_rope), pos)
    o = jnp.einsum('bqhc,chd->bqhd', o_lat, w_uv).reshape(b, s, MLA_HEADS * MLA_V)
    return o @ w_o, c_kv, k_rope


def setup_inputs(seed: int = 0) -> dict:
    key = jax.random.key(seed)
    keys = iter(jax.random.split(key, 64))
    def normal(shape, scale):
        return scale * jax.random.normal(next(keys), shape, jnp.float32)
    def weight(shape, fan_in):
        return normal(shape, fan_in ** -0.5)
    def gain(shape):
        return 1.0 + normal(shape, 0.02)
    n_pages = PAST_LEN // PAGE_SIZE
    n_pool = (5 * DEC_BATCH * n_pages) // 4
    w_buf = min(WINDOW, PAST_LEN)
    perm = jax.random.permutation(next(keys), n_pool)
    page_table = perm[:DEC_BATCH * n_pages].reshape(DEC_BATCH, n_pages).astype(jnp.int32)
    diff_out_in = DIFF_HEADS * 2 * DIFF_HEAD_DIM
    return {
        'x_prompt': normal((BATCH, SEQ, D_MODEL), 1.0),
        'x_sample': normal((DEC_BATCH, DEC_SEQ, D_MODEL), 1.0),
        'cache_diff_k': normal((N_A, n_pool, PAGE_SIZE, DIFF_KV_HEADS, 2 * DIFF_HEAD_DIM), 1.0),
        'cache_diff_v': normal((N_A, n_pool, PAGE_SIZE, DIFF_KV_HEADS, 2 * DIFF_HEAD_DIM), 1.0),
        'cache_swa_k': normal((N_B, DEC_BATCH, w_buf, SWA_KV_HEADS, SWA_HEAD_DIM), 1.0),
        'cache_swa_v': normal((N_B, DEC_BATCH, w_buf, SWA_KV_HEADS, SWA_HEAD_DIM), 1.0),
        'cache_mla_latent': normal((N_C, n_pool, PAGE_SIZE, MLA_KV_LORA), 1.0),
        'cache_mla_krope': normal((N_C, n_pool, PAGE_SIZE, MLA_ROPE), 1.0),
        'page_table': page_table,
        'norm_ffn_pre': gain((DEPTH, D_MODEL)),
        'ffn_pre_gate': weight((DEPTH, D_MODEL, FFN_DIM), D_MODEL),
        'ffn_pre_up': weight((DEPTH, D_MODEL, FFN_DIM), D_MODEL),
        'ffn_pre_down': weight((DEPTH, FFN_DIM, D_MODEL), FFN_DIM),
        'norm_mix': gain((DEPTH, D_MODEL)),
        'norm_ffn_post': gain((DEPTH, D_MODEL)),
        'ffn_post_gate': weight((DEPTH, D_MODEL, FFN_DIM), D_MODEL),
        'ffn_post_up': weight((DEPTH, D_MODEL, FFN_DIM), D_MODEL),
        'ffn_post_down': weight((DEPTH, FFN_DIM, D_MODEL), FFN_DIM),
        'diff_w_qkv': weight((N_A, D_MODEL, DIFF_QKV_DIM), D_MODEL),
        'diff_q_gain': gain((N_A, DIFF_HEAD_DIM)),
        'diff_k_gain': gain((N_A, DIFF_HEAD_DIM)),
        'diff_lambda_q1': normal((N_A, DIFF_HEAD_DIM), 0.1),
        'diff_lambda_k1': normal((N_A, DIFF_HEAD_DIM), 0.1),
        'diff_lambda_q2': normal((N_A, DIFF_HEAD_DIM), 0.1),
        'diff_lambda_k2': normal((N_A, DIFF_HEAD_DIM), 0.1),
        'diff_sub_gain': gain((N_A, 2 * DIFF_HEAD_DIM)),
        'diff_w_o': weight((N_A, diff_out_in, D_MODEL), diff_out_in),
        'swa_w_qkv': weight((N_B, D_MODEL, SWA_QKV_DIM), D_MODEL),
        'swa_q_gain': gain((N_B, SWA_HEAD_DIM)),
        'swa_k_gain': gain((N_B, SWA_HEAD_DIM)),
        'swa_sinks': normal((N_B, SWA_HEADS), 1.0),
        'swa_w_o': weight((N_B, SWA_HEADS * SWA_HEAD_DIM, D_MODEL), SWA_HEADS * SWA_HEAD_DIM),
        'mla_w_down': weight((N_C, D_MODEL, MLA_DOWN_DIM), D_MODEL),
        'mla_q_a_gain': gain((N_C, MLA_Q_LORA)),
        'mla_kv_a_gain': gain((N_C, MLA_KV_LORA)),
        'mla_w_uq': weight((N_C, MLA_Q_LORA, MLA_HEADS * (MLA_NOPE + MLA_ROPE)), MLA_Q_LORA),
        'mla_qn_gain': gain((N_C, MLA_NOPE)),
        'mla_qr_gain': gain((N_C, MLA_ROPE)),
        'mla_kn_gain': gain((N_C, MLA_NOPE)),
        'mla_kr_gain': gain((N_C, MLA_ROPE)),
        'mla_w_uk': weight((N_C, MLA_KV_LORA, MLA_HEADS, MLA_NOPE), MLA_KV_LORA),
        'mla_w_uv': weight((N_C, MLA_KV_LORA, MLA_HEADS, MLA_V), MLA_KV_LORA),
        'mla_w_o': weight((N_C, MLA_HEADS * MLA_V, D_MODEL), MLA_HEADS * MLA_V),
    }


def reference(x_prompt, x_sample, cache_diff_k, cache_diff_v, cache_swa_k, cache_swa_v,
              cache_mla_latent, cache_mla_krope, page_table,
              norm_ffn_pre, ffn_pre_gate, ffn_pre_up, ffn_pre_down, norm_mix, norm_ffn_post,
              ffn_post_gate, ffn_post_up, ffn_post_down,
              diff_w_qkv, diff_q_gain, diff_k_gain, diff_lambda_q1, diff_lambda_k1, diff_lambda_q2,
              diff_lambda_k2, diff_sub_gain, diff_w_o,
              swa_w_qkv, swa_q_gain, swa_k_gain, swa_sinks, swa_w_o,
              mla_w_down, mla_q_a_gain, mla_kv_a_gain, mla_w_uq, mla_qn_gain, mla_qr_gain,
              mla_kn_gain, mla_kr_gain, mla_w_uk, mla_w_uv, mla_w_o):
    pos_p = jnp.arange(x_prompt.shape[1], dtype=jnp.int32)
    pos_s = PAST_LEN + jnp.arange(x_sample.shape[1], dtype=jnp.int32)
    xp, xs = x_prompt, x_sample
    dk_p, dv_p, dk_s, dv_s = [], [], [], []
    wk_p, wv_p, wk_s, wv_s = [], [], [], []
    ml_p, mr_p, ml_s, mr_s = [], [], [], []
    for i in range(DEPTH):
        kind, j = i % N_MIXERS, i // N_MIXERS
        ffn_a = (ffn_pre_gate[i], ffn_pre_up[i], ffn_pre_down[i])
        xp = xp + 0.5 * swiglu(rms_norm(xp, norm_ffn_pre[i]), *ffn_a)
        xs = xs + 0.5 * swiglu(rms_norm(xs, norm_ffn_pre[i]), *ffn_a)
        hp, hs = rms_norm(xp, norm_mix[i]), rms_norm(xs, norm_mix[i])
        if kind == 0:
            lam_init = 0.8 - 0.6 * math.exp(-0.3 * i)
            prm = (diff_w_qkv[j], diff_q_gain[j], diff_k_gain[j], diff_lambda_q1[j], diff_lambda_k1[j],
                   diff_lambda_q2[j], diff_lambda_k2[j], diff_sub_gain[j], diff_w_o[j])
            op, k_new, v_new = diff_attention(hp, pos_p, None, None, *prm, lam_init)
            dk_p.append(k_new)
            dv_p.append(v_new)
            os_, k_new, v_new = diff_attention(hs, pos_s, gather_pages(cache_diff_k[j], page_table),
                                               gather_pages(cache_diff_v[j], page_table), *prm, lam_init)
            dk_s.append(k_new)
            dv_s.append(v_new)
        elif kind == 1:
            prm = (swa_w_qkv[j], swa_q_gain[j], swa_k_gain[j], swa_sinks[j], swa_w_o[j])
            op, k_new, v_new = sliding_window_attention(hp, pos_p, None, None, *prm)
            wk_p.append(k_new)
            wv_p.append(v_new)
            os_, k_new, v_new = sliding_window_attention(hs, pos_s, cache_swa_k[j], cache_swa_v[j], *prm)
            wk_s.append(k_new)
            wv_s.append(v_new)
        else:
            prm = (mla_w_down[j], mla_q_a_gain[j], mla_kv_a_gain[j], mla_w_uq[j], mla_qn_gain[j],
                   mla_qr_gain[j], mla_kn_gain[j], mla_kr_gain[j], mla_w_uk[j], mla_w_uv[j], mla_w_o[j])
            op, lat_new, kr_new = latent_attention(hp, pos_p, None, None, *prm)
            ml_p.append(lat_new)
            mr_p.append(kr_new)
            os_, lat_new, kr_new = latent_attention(hs, pos_s, gather_pages(cache_mla_latent[j], page_table),
                                                    gather_pages(cache_mla_krope[j], page_table), *prm)
            ml_s.append(lat_new)
            mr_s.append(kr_new)
        xp = xp + op
        xs = xs + os_
        ffn_b = (ffn_post_gate[i], ffn_post_up[i], ffn_post_down[i])
        xp = xp + 0.5 * swiglu(rms_norm(xp, norm_ffn_post[i]), *ffn_b)
        xs = xs + 0.5 * swiglu(rms_norm(xs, norm_ffn_post[i]), *ffn_b)
    return (xp, xs,
            jnp.stack(dk_p), jnp.stack(dv_p), jnp.stack(dk_s), jnp.stack(dv_s),
            jnp.stack(wk_p), jnp.stack(wv_p), jnp.stack(wk_s), jnp.stack(wv_s),
            jnp.stack(ml_p), jnp.stack(mr_p), jnp.stack(ml_s), jnp.stack(mr_s))
```

```python
import functools
import math

import jax
import jax.numpy as jnp
from jax import lax
from jax.experimental import pallas as pl
from jax.experimental.pallas import tpu as pltpu

NORM_EPS = 1e-6
ROPE_THETA = 10000.0
WINDOW = 128
LANES = 128
VMEM_LIMIT_BYTES = 56 * 1024 * 1024
DECODE_PAGES_PER_STEP = 8
BF16 = jnp.bfloat16
F32 = jnp.float32
NEG_INF = float("-inf")


def _pick(n, cap, mults=(128, 16, 8)):
    for mult in mults:
        best = 0
        for d in range(mult, min(n, cap) + 1, mult):
            if n % d == 0:
                best = d
        if best:
            return best
    return n


def _params(*sem):
    return pltpu.CompilerParams(dimension_semantics=sem, vmem_limit_bytes=VMEM_LIMIT_BYTES)


def _rms(x, gain):
    ms = jnp.mean(x * x, axis=-1, keepdims=True)
    return x * lax.rsqrt(ms + NORM_EPS) * gain


def _ffn_kernel(x_ref, g_ref, wg_ref, wu_ref, wd_ref, o_ref, h_ref):
    @pl.when(pl.program_id(1) == 0)
    def _():
        x = x_ref[...]
        h_ref[...] = _rms(x, g_ref[...]).astype(BF16)
        o_ref[...] = x

    h = h_ref[...]
    g = jnp.dot(h, wg_ref[...], preferred_element_type=F32)
    u = jnp.dot(h, wu_ref[...], preferred_element_type=F32)
    act = g * (0.5 / (1.0 + jnp.exp(-g))) * u
    o_ref[...] += jnp.dot(act.astype(BF16), wd_ref[...], preferred_element_type=F32)


def _ffn(x, gain, wg, wu, wd):
    t, d = x.shape
    f = wg.shape[1]
    tm = _pick(t, 768)
    tf = _pick(f, 512)
    return pl.pallas_call(
        _ffn_kernel,
        grid=(t // tm, f // tf),
        in_specs=[
            pl.BlockSpec((tm, d), lambda i, j: (i, 0)),
            pl.BlockSpec((1, d), lambda i, j: (0, 0)),
            pl.BlockSpec((d, tf), lambda i, j: (0, j)),
            pl.BlockSpec((d, tf), lambda i, j: (0, j)),
            pl.BlockSpec((tf, d), lambda i, j: (j, 0)),
        ],
        out_specs=pl.BlockSpec((tm, d), lambda i, j: (i, 0)),
        out_shape=jax.ShapeDtypeStruct((t, d), F32),
        scratch_shapes=[pltpu.VMEM((tm, d), BF16)],
        compiler_params=_params("parallel", "arbitrary"),
        name="ffn",
    )(x, gain.reshape(1, d), wg, wu, wd)


def _norm_mm_kernel(x_ref, g_ref, w_ref, o_ref, h_ref):
    @pl.when(pl.program_id(1) == 0)
    def _():
        h_ref[...] = _rms(x_ref[...], g_ref[...]).astype(BF16)

    o_ref[...] = jnp.dot(h_ref[...], w_ref[...], preferred_element_type=F32)


def _norm_mm(x, gain, w, k):
    t = x.shape[0]
    n = w.shape[1]
    tm = _pick(t, 768)
    tn = _pick(n, 1024)
    return pl.pallas_call(
        _norm_mm_kernel,
        grid=(t // tm, n // tn),
        in_specs=[
            pl.BlockSpec((tm, k), lambda i, j: (i, 0)),
            pl.BlockSpec((1, k), lambda i, j: (0, 0)),
            pl.BlockSpec((k, tn), lambda i, j: (0, j)),
        ],
        out_specs=pl.BlockSpec((tm, tn), lambda i, j: (i, j)),
        out_shape=jax.ShapeDtypeStruct((t, n), F32),
        scratch_shapes=[pltpu.VMEM((tm, k), BF16)],
        compiler_params=_params("parallel", "arbitrary"),
        name="norm_mm",
    )(x, gain.reshape(1, k), w)


def _mm_res_kernel(a_ref, w_ref, x_ref, o_ref):
    o_ref[...] = x_ref[...] + jnp.dot(a_ref[...], w_ref[...], preferred_element_type=F32)


def _mm_res(a, w, x):
    t, k = a.shape
    n = w.shape[1]
    tm = _pick(t, 768)
    tn = _pick(n, 1024)
    return pl.pallas_call(
        _mm_res_kernel,
        grid=(t // tm, n // tn),
        in_specs=[
            pl.BlockSpec((tm, k), lambda i, j: (i, 0)),
            pl.BlockSpec((k, tn), lambda i, j: (0, j)),
            pl.BlockSpec((tm, tn), lambda i, j: (i, j)),
        ],
        out_specs=pl.BlockSpec((tm, tn), lambda i, j: (i, j)),
        out_shape=jax.ShapeDtypeStruct((t, n), F32),
        compiler_params=_params("parallel", "parallel"),
        name="mm_res",
    )(a, w, x)


def _head_norm(x, gain, group):
    xx = x * x
    if group == LANES:
        ms = jnp.mean(xx, axis=-1, keepdims=True)
    else:
        lo = lax.broadcasted_iota(jnp.int32, x.shape, 1) < group
        s_lo = jnp.sum(jnp.where(lo, xx, 0.0), axis=-1, keepdims=True)
        s_hi = jnp.sum(jnp.where(lo, 0.0, xx), axis=-1, keepdims=True)
        ms = jnp.where(lo, s_lo, s_hi) * (1.0 / group)
    return x * lax.rsqrt(ms + NORM_EPS) * gain


def _rope(y, cosf, sinf, group):
    if group == LANES:
        rot = pltpu.roll(y, LANES // 2, 1)
    else:
        lane = lax.broadcasted_iota(jnp.int32, y.shape, 1)
        first_half = (lane % group) < (group // 2)
        rot = jnp.where(first_half, pltpu.roll(y, LANES - group // 2, 1), pltpu.roll(y, group // 2, 1))
    return y * cosf + rot * sinf


def _diff_prep_kernel(qkv_ref, cos_ref, sin_ref, qg_ref, kg_ref, q_ref, k_ref, v_ref, *, nq, nk, scale):
    cosf, sinf = cos_ref[...], sin_ref[...]
    for c in range(nq):
        y = _head_norm(qkv_ref[:, c * LANES:(c + 1) * LANES], qg_ref[...], LANES)
        q_ref[:, c * LANES:(c + 1) * LANES] = (_rope(y, cosf, sinf, LANES) * scale).astype(BF16)
    for c in range(nk):
        y = _head_norm(qkv_ref[:, (nq + c) * LANES:(nq + c + 1) * LANES], kg_ref[...], LANES)
        k_ref[:, c * LANES:(c + 1) * LANES] = _rope(y, cosf, sinf, LANES)
    v_ref[...] = qkv_ref[:, (nq + nk) * LANES:]


def _diff_prep(qkv, cosf, sinf, q_gain, k_gain, nq_cols, nk_cols, scale):
    t, n = qkv.shape
    tm = _pick(t, 768)
    nv_cols = n - nq_cols - nk_cols
    row = lambda i: (i, 0)
    fixed = lambda i: (0, 0)
    return pl.pallas_call(
        functools.partial(_diff_prep_kernel, nq=nq_cols // LANES, nk=nk_cols // LANES, scale=scale),
        grid=(t // tm,),
        in_specs=[
            pl.BlockSpec((tm, n), row),
            pl.BlockSpec((tm, LANES), row),
            pl.BlockSpec((tm, LANES), row),
            pl.BlockSpec((1, LANES), fixed),
            pl.BlockSpec((1, LANES), fixed),
        ],
        out_specs=[
            pl.BlockSpec((tm, nq_cols), row),
            pl.BlockSpec((tm, nk_cols), row),
            pl.BlockSpec((tm, nv_cols), row),
        ],
        out_shape=[
            jax.ShapeDtypeStruct((t, nq_cols), BF16),
            jax.ShapeDtypeStruct((t, nk_cols), F32),
            jax.ShapeDtypeStruct((t, nv_cols), F32),
        ],
        compiler_params=_params("parallel"),
        name="diff_prep",
    )(qkv, cosf, sinf, q_gain.reshape(1, LANES), k_gain.reshape(1, LANES))


def _swa_prep_kernel(qkv_ref, cos_ref, sin_ref, qg_ref, kg_ref, q_ref, k_ref, v_ref, *, nq, nk, group, scale):
    cosf, sinf = cos_ref[...], sin_ref[...]
    for c in range(nq):
        y = _head_norm(qkv_ref[:, c * LANES:(c + 1) * LANES], qg_ref[...], group)
        q_ref[:, c * LANES:(c + 1) * LANES] = (_rope(y, cosf, sinf, group) * scale).astype(BF16)
    for c in range(nk):
        y = _head_norm(qkv_ref[:, (nq + c) * LANES:(nq + c + 1) * LANES], kg_ref[...], group)
        k_ref[:, c * LANES:(c + 1) * LANES] = _rope(y, cosf, sinf, group)
    v_ref[...] = qkv_ref[:, (nq + nk) * LANES:]


def _swa_prep(qkv, cosf, sinf, q_gain, k_gain, nq_cols, nk_cols, group, scale):
    t, n = qkv.shape
    tm = _pick(t, 768)
    nv_cols = n - nq_cols - nk_cols
    rep = LANES // group
    row = lambda i: (i, 0)
    fixed = lambda i: (0, 0)
    return pl.pallas_call(
        functools.partial(_swa_prep_kernel, nq=nq_cols // LANES, nk=nk_cols // LANES, group=group, scale=scale),
        grid=(t // tm,),
        in_specs=[
            pl.BlockSpec((tm, n), row),
            pl.BlockSpec((tm, LANES), row),
            pl.BlockSpec((tm, LANES), row),
            pl.BlockSpec((1, LANES), fixed),
            pl.BlockSpec((1, LANES), fixed),
        ],
        out_specs=[
            pl.BlockSpec((tm, nq_cols), row),
            pl.BlockSpec((tm, nk_cols), row),
            pl.BlockSpec((tm, nv_cols), row),
        ],
        out_shape=[
            jax.ShapeDtypeStruct((t, nq_cols), BF16),
            jax.ShapeDtypeStruct((t, nk_cols), F32),
            jax.ShapeDtypeStruct((t, nv_cols), F32),
        ],
        compiler_params=_params("parallel"),
        name="swa_prep",
    )(qkv, cosf, sinf, jnp.tile(q_gain, rep).reshape(1, LANES), jnp.tile(k_gain, rep).reshape(1, LANES))


def _mla_down_prep_kernel(d_ref, cos_ref, sin_ref, kvg_ref, krg_ref, lat_ref, latb_ref, kr_ref, *, q_lora, kv_lora, rope):
    lat = _rms(d_ref[:, q_lora:q_lora + kv_lora], kvg_ref[...])
    lat_ref[...] = lat
    latb_ref[...] = lat.astype(BF16)
    y = _head_norm(d_ref[:, q_lora + kv_lora:q_lora + kv_lora + LANES], krg_ref[...], rope)
    kr_ref[...] = _rope(y, cos_ref[...], sin_ref[...], rope)


def _mla_down_prep(d, cosf, sinf, kv_gain, kr_gain, q_lora, kv_lora, rope):
    t, n = d.shape
    tm = _pick(t, 768)
    row = lambda i: (i, 0)
    fixed = lambda i: (0, 0)
    krg = jnp.concatenate([kr_gain, jnp.zeros((LANES - rope,), F32)]).reshape(1, LANES)
    return pl.pallas_call(
        functools.partial(_mla_down_prep_kernel, q_lora=q_lora, kv_lora=kv_lora, rope=rope),
        grid=(t // tm,),
        in_specs=[
            pl.BlockSpec((tm, n), row),
            pl.BlockSpec((tm, LANES), row),
            pl.BlockSpec((tm, LANES), row),
            pl.BlockSpec((1, kv_lora), fixed),
            pl.BlockSpec((1, LANES), fixed),
        ],
        out_specs=[
            pl.BlockSpec((tm, kv_lora), row),
            pl.BlockSpec((tm, kv_lora), row),
            pl.BlockSpec((tm, LANES), row),
        ],
        out_shape=[
            jax.ShapeDtypeStruct((t, kv_lora), F32),
            jax.ShapeDtypeStruct((t, kv_lora), BF16),
            jax.ShapeDtypeStruct((t, LANES), F32),
        ],
        compiler_params=_params("parallel"),
        name="mla_down_prep",
    )(d, cosf, sinf, kv_gain.reshape(1, kv_lora), krg)


def _mla_q_prep_kernel(q_ref, cos_ref, sin_ref, ng_ref, rg_ref, o_ref, *, heads, rope, scale):
    cosf, sinf = cos_ref[...], sin_ref[...]
    for h in range(heads):
        lo = 2 * h * LANES
        y = _head_norm(q_ref[:, lo:lo + LANES], ng_ref[...], LANES)
        o_ref[:, lo:lo + LANES] = (y * scale).astype(BF16)
        y = _head_norm(q_ref[:, lo + LANES:lo + 2 * LANES], rg_ref[...], rope)
        o_ref[:, lo + LANES:lo + 2 * LANES] = (_rope(y, cosf, sinf, rope) * scale).astype(BF16)


def _mla_q_prep(q, cosf, sinf, qn_gain, qr_gain, heads, rope, scale):
    t, n = q.shape
    tm = _pick(t, 768)
    row = lambda i: (i, 0)
    fixed = lambda i: (0, 0)
    rg = jnp.concatenate([qr_gain, jnp.zeros((LANES - rope,), F32)]).reshape(1, LANES)
    return pl.pallas_call(
        functools.partial(_mla_q_prep_kernel, heads=heads, rope=rope, scale=scale),
        grid=(t // tm,),
        in_specs=[
            pl.BlockSpec((tm, n), row),
            pl.BlockSpec((tm, LANES), row),
            pl.BlockSpec((tm, LANES), row),
            pl.BlockSpec((1, LANES), fixed),
            pl.BlockSpec((1, LANES), fixed),
        ],
        out_specs=pl.BlockSpec((tm, n), row),
        out_shape=jax.ShapeDtypeStruct((t, n), BF16),
        compiler_params=_params("parallel"),
        name="mla_q_prep",
    )(q, cosf, sinf, qn_gain.reshape(1, LANES), rg)


def _mla_k_prep_kernel(lat_ref, kr_ref, w_ref, g_ref, o_ref, *, heads):
    y = jnp.dot(lat_ref[...], w_ref[...], preferred_element_type=F32)
    kr = kr_ref[...].astype(BF16)
    for h in range(heads):
        o_ref[:, 2 * h * LANES:(2 * h + 1) * LANES] = _rms(y[:, h * LANES:(h + 1) * LANES], g_ref[...]).astype(BF16)
        o_ref[:, (2 * h + 1) * LANES:(2 * h + 2) * LANES] = kr


def _mla_k_prep(latb, kr_pad, w_uk, kn_gain, heads):
    t, c = latb.shape
    tm = _pick(t, 768)
    row = lambda i: (i, 0)
    fixed = lambda i: (0, 0)
    return pl.pallas_call(
        functools.partial(_mla_k_prep_kernel, heads=heads),
        grid=(t // tm,),
        in_specs=[
            pl.BlockSpec((tm, c), row),
            pl.BlockSpec((tm, LANES), row),
            pl.BlockSpec((c, heads * LANES), fixed),
            pl.BlockSpec((1, LANES), fixed),
        ],
        out_specs=pl.BlockSpec((tm, 2 * heads * LANES), row),
        out_shape=jax.ShapeDtypeStruct((t, 2 * heads * LANES), BF16),
        compiler_params=_params("parallel"),
        name="mla_k_prep",
    )(latb, kr_pad, w_uk, kn_gain.reshape(1, LANES))


def _online_update(s, v, m_ref, l_ref, acc_ref, idx):
    m_prev = m_ref[idx]
    m_new = jnp.maximum(m_prev, jnp.max(s, axis=-1, keepdims=True))
    alpha = jnp.exp(m_prev - m_new)
    p = jnp.exp(s - m_new)
    l_ref[idx] = alpha * l_ref[idx] + jnp.sum(p, axis=-1, keepdims=True)
    acc_ref[idx] = alpha * acc_ref[idx] + jnp.dot(p.astype(BF16), v, preferred_element_type=F32)
    m_ref[idx] = m_new


def _qk(q, k):
    return lax.dot_general(q, k, (((1,), (1,)), ((), ())), preferred_element_type=F32)


def _diff_lambda(lq1, lk1, lq2, lk2, lam_init):
    return (jnp.exp(jnp.sum(lq1[...] * lk1[...], axis=-1, keepdims=True))
            - jnp.exp(jnp.sum(lq2[...] * lk2[...], axis=-1, keepdims=True)) + lam_init)


def _diff_attn_kernel(q_ref, k_ref, v_ref, lq1, lk1, lq2, lk2, sg_ref, o_ref, m_ref, l_ref, acc_ref,
                      *, tq, group, lam_init):
    qi = pl.program_id(2)
    m_ref[...] = jnp.full(m_ref.shape, NEG_INF, F32)
    l_ref[...] = jnp.zeros(l_ref.shape, F32)
    acc_ref[...] = jnp.zeros(acc_ref.shape, F32)
    hd = LANES
    causal = (lax.broadcasted_iota(jnp.int32, (tq, tq), 0) >= lax.broadcasted_iota(jnp.int32, (tq, tq), 1))

    def step(ki, masked):
        ks = pl.multiple_of(ki * tq, tq)
        kb = k_ref[pl.ds(ks, tq), :].astype(BF16)
        vb = v_ref[pl.ds(ks, tq), :].astype(BF16)
        for idx in range(2 * group):
            c = idx % 2
            s = _qk(q_ref[:, idx * hd:(idx + 1) * hd], kb[:, c * hd:(c + 1) * hd])
            if masked:
                s = jnp.where(causal, s, NEG_INF)
            _online_update(s, vb, m_ref, l_ref, acc_ref, idx)

    def body(ki, carry):
        step(ki, False)
        return carry

    lax.fori_loop(0, qi, body, 0)
    step(qi, True)

    lam = _diff_lambda(lq1, lk1, lq2, lk2, lam_init)
    for g in range(group):
        o = acc_ref[2 * g] / l_ref[2 * g] - lam * (acc_ref[2 * g + 1] / l_ref[2 * g + 1])
        o = _rms(o, sg_ref[...]) * (1.0 - lam_init)
        o_ref[:, g * 2 * hd:(g + 1) * 2 * hd] = o.astype(BF16)


def _diff_attn_prompt(q, k, v, lam_prm, sub_gain, b, s, kv_heads, group, lam_init):
    hd = LANES
    tq = _pick(s, 512, (128,))
    nq = s // tq
    qw = group * 2 * hd
    vec = lambda *_: (0, 0)
    return pl.pallas_call(
        functools.partial(_diff_attn_kernel, tq=tq, group=group, lam_init=lam_init),
        grid=(b, kv_heads, nq),
        in_specs=[
            pl.BlockSpec((tq, qw), lambda bi, h, qi: (bi * nq + qi, h)),
            pl.BlockSpec((s, 2 * hd), lambda bi, h, qi: (bi, h)),
            pl.BlockSpec((s, 2 * hd), lambda bi, h, qi: (bi, h)),
            pl.BlockSpec((1, hd), vec), pl.BlockSpec((1, hd), vec),
            pl.BlockSpec((1, hd), vec), pl.BlockSpec((1, hd), vec),
            pl.BlockSpec((1, 2 * hd), vec),
        ],
        out_specs=pl.BlockSpec((tq, qw), lambda bi, h, qi: (bi * nq + qi, h)),
        out_shape=jax.ShapeDtypeStruct((b * s, kv_heads * qw), BF16),
        scratch_shapes=[
            pltpu.VMEM((2 * group, tq, 1), F32),
            pltpu.VMEM((2 * group, tq, 1), F32),
            pltpu.VMEM((2 * group, tq, 2 * hd), F32),
        ],
        compiler_params=_params("parallel", "parallel", "arbitrary"),
        name="diff_attn_prompt",
    )(q, k, v, *[p.reshape(1, hd) for p in lam_prm], sub_gain.reshape(1, 2 * hd))


def _diff_decode_kernel(pt_ref, qm_ref, *refs, pages, page, kv_heads, group, lam_init):
    k_pages = refs[:pages]
    v_pages = refs[pages:2 * pages]
    kn_ref, vn_ref, lq1, lk1, lq2, lk2, sg_ref, o_ref, m_ref, l_ref, acc_ref = refs[2 * pages:]
    i = pl.program_id(1)
    hw = 2 * LANES

    def head_rows(page_refs, h):
        halves = [jnp.concatenate([r[pl.ds(c * kv_heads + h, page, stride=2 * kv_heads), :] for r in page_refs], axis=0)
                  for c in range(2)]
        return jnp.concatenate(halves, axis=1).astype(BF16)

    @pl.when(i == 0)
    def _():
        m_ref[...] = jnp.full(m_ref.shape, NEG_INF, F32)
        l_ref[...] = jnp.zeros(l_ref.shape, F32)
        acc_ref[...] = jnp.zeros(acc_ref.shape, F32)

    for h in range(kv_heads):
        _online_update(_qk(qm_ref[h], head_rows(k_pages, h)), head_rows(v_pages, h), m_ref, l_ref, acc_ref, h)

    @pl.when(i == pl.num_programs(1) - 1)
    def _():
        first = lax.broadcasted_iota(jnp.int32, (LANES, hw), 0) == 0
        lam = _diff_lambda(lq1, lk1, lq2, lk2, lam_init)
        for h in range(kv_heads):
            kb = jnp.where(first, kn_ref[:, h * hw:(h + 1) * hw], 0.0).astype(BF16)
            vb = jnp.where(first, vn_ref[:, h * hw:(h + 1) * hw], 0.0).astype(BF16)
            s = _qk(qm_ref[h], kb)
            s = jnp.where(lax.broadcasted_iota(jnp.int32, s.shape, 1) == 0, s, NEG_INF)
            _online_update(s, vb, m_ref, l_ref, acc_ref, h)
            a = acc_ref[h] / l_ref[h]
            for g in range(group):
                o = a[2 * g:2 * g + 1] - lam * a[2 * g + 1:2 * g + 2]
                o = _rms(o, sg_ref[...]) * (1.0 - lam_init)
                col = (h * group + g) * hw
                o_ref[:, col:col + hw] = o.astype(BF16)


def _diff_decode(qm, k_cache, v_cache, layer, page_table, k_new, v_new, lam_prm, sub_gain, group, lam_init):
    db, kv_heads = qm.shape[0], qm.shape[1]
    n_pages = page_table.shape[1]
    hw = 2 * LANES
    width = kv_heads * hw
    page = k_cache.shape[2]
    pages = math.gcd(n_pages, DECODE_PAGES_PER_STEP)
    layers, pool = k_cache.shape[0], k_cache.shape[1]
    rows = page * 2 * kv_heads

    def page_rows(cache):
        c = cache.reshape(layers, pool, page, kv_heads, 2, LANES)
        return c.transpose(0, 1, 2, 4, 3, 5).reshape(layers, pool, rows, LANES)

    kc, vc = page_rows(k_cache), page_rows(v_cache)

    def page_spec(j):
        return pl.BlockSpec((None, None, rows, LANES),
                            lambda b, i, pt: (layer, pt[b * n_pages + i * pages + j], 0, 0))

    row3 = lambda b, i, pt: (b, 0, 0)
    vec = lambda b, i, pt: (0, 0)
    grid_spec = pltpu.PrefetchScalarGridSpec(
        num_scalar_prefetch=1,
        grid=(db, n_pages // pages),
        in_specs=(
            [pl.BlockSpec((None, kv_heads, 2 * group, hw), lambda b, i, pt: (b, 0, 0, 0))]
            + [page_spec(j) for j in range(pages)] + [page_spec(j) for j in range(pages)]
            + [pl.BlockSpec((None, 1, width), row3), pl.BlockSpec((None, 1, width), row3)]
            + [pl.BlockSpec((1, LANES), vec)] * 4 + [pl.BlockSpec((1, hw), vec)]
        ),
        out_specs=pl.BlockSpec((None, 1, kv_heads * group * hw), row3),
        scratch_shapes=[
            pltpu.VMEM((kv_heads, 2 * group, 1), F32),
            pltpu.VMEM((kv_heads, 2 * group, 1), F32),
            pltpu.VMEM((kv_heads, 2 * group, hw), F32),
        ],
    )
    return pl.pallas_call(
        functools.partial(_diff_decode_kernel, pages=pages, page=page, kv_heads=kv_heads, group=group,
                          lam_init=lam_init),
        grid_spec=grid_spec,
        out_shape=jax.ShapeDtypeStruct((db, 1, kv_heads * group * hw), BF16),
        compiler_params=_params("parallel", "arbitrary"),
        name="diff_decode",
    )(page_table.reshape(-1), qm, *([kc] * pages), *([vc] * pages),
      k_new.reshape(db, 1, width), v_new.reshape(db, 1, width),
      *[p.reshape(1, LANES) for p in lam_prm], sub_gain.reshape(1, hw))


def _swa_attn_kernel(q_ref, kp_ref, kc_ref, vp_ref, vc_ref, sink_ref, o_ref, *, kv_heads, group, hd, win):
    blk = pl.program_id(1)
    rows = group * win
    r = lax.broadcasted_iota(jnp.int32, (rows, 2 * win), 0) % win
    c = lax.broadcasted_iota(jnp.int32, (rows, 2 * win), 1)
    dist = win + r - c
    mask = (dist >= 0) & (dist <= win) & ((c >= win) | (blk > 0))
    for j in range(kv_heads):
        q = jnp.concatenate([q_ref[:, (j * group + g) * hd:(j * group + g + 1) * hd] for g in range(group)], axis=0)
        k = jnp.concatenate([kp_ref[:, j * hd:(j + 1) * hd], kc_ref[:, j * hd:(j + 1) * hd]], axis=0).astype(BF16)
        v = jnp.concatenate([vp_ref[:, j * hd:(j + 1) * hd], vc_ref[:, j * hd:(j + 1) * hd]], axis=0).astype(BF16)
        s = jnp.where(mask, _qk(q, k), NEG_INF)
        sink = jnp.concatenate([jnp.broadcast_to(sink_ref[j * group + g:j * group + g + 1, :], (win, 1))
                                for g in range(group)], axis=0)
        m = jnp.maximum(jnp.max(s, axis=-1, keepdims=True), sink)
        p = jnp.exp(s - m)
        den = jnp.sum(p, axis=-1, keepdims=True) + jnp.exp(sink - m)
        o = jnp.dot((p / den).astype(BF16), v, preferred_element_type=F32)
        for g in range(group):
            o_ref[:, (j * group + g) * hd:(j * group + g + 1) * hd] = o[g * win:(g + 1) * win].astype(BF16)


def _swa_attn_prompt(q, k, v, sinks, b, s, kv_heads, group, hd):
    win = WINDOW
    nb = s // win
    heads = kv_heads * group
    cur = lambda bi, i: (bi * nb + i, 0)
    prev = lambda bi, i: (bi * nb + jnp.maximum(i - 1, 0), 0)
    return pl.pallas_call(
        functools.partial(_swa_attn_kernel, kv_heads=kv_heads, group=group, hd=hd, win=win),
        grid=(b, nb),
        in_specs=[
            pl.BlockSpec((win, heads * hd), cur),
            pl.BlockSpec((win, kv_heads * hd), prev),
            pl.BlockSpec((win, kv_heads * hd), cur),
            pl.BlockSpec((win, kv_heads * hd), prev),
            pl.BlockSpec((win, kv_heads * hd), cur),
            pl.BlockSpec((heads, 1), lambda bi, i: (0, 0)),
        ],
        out_specs=pl.BlockSpec((win, heads * hd), cur),
        out_shape=jax.ShapeDtypeStruct((b * s, heads * hd), BF16),
        compiler_params=_params("parallel", "parallel"),
        name="swa_attn_prompt",
    )(q, k, k, v, v, sinks.reshape(heads, 1))


def _swa_decode_kernel(q_ref, k_ref, v_ref, kn_ref, vn_ref, sink_ref, o_ref, *, bb, kv_heads, group, hd):
    heads = kv_heads * group
    width = kv_heads * hd
    own = (lax.broadcasted_iota(jnp.int32, (heads, width), 1) // hd
           == lax.broadcasted_iota(jnp.int32, (heads, width), 0) // group)
    sink = sink_ref[...]
    for bi in range(bb):
        q = q_ref[bi]
        s = jnp.dot(q, k_ref[bi].astype(BF16), preferred_element_type=F32)
        s_new = jnp.sum(q.astype(F32) * kn_ref[bi:bi + 1, :], axis=-1, keepdims=True)
        m = jnp.maximum(jnp.maximum(jnp.max(s, axis=-1, keepdims=True), s_new), sink)
        p = jnp.exp(s - m)
        p_new = jnp.exp(s_new - m)
        den = jnp.sum(p, axis=-1, keepdims=True) + p_new + jnp.exp(sink - m)
        o = _qk(p.astype(BF16), v_ref[bi].astype(BF16)) + p_new * vn_ref[bi:bi + 1, :]
        o = jnp.where(own, o / den, 0.0)
        acc = o[:, 0:hd]
        for j in range(1, kv_heads):
            acc = acc + o[:, j * hd:(j + 1) * hd]
        o_ref[bi] = acc.astype(BF16)


def _swa_decode(qbd, k_cache, v_cache, layer, k_new, v_new, sinks, kv_heads, group, hd):
    db, heads, width = qbd.shape
    win = k_cache.shape[2]
    bb = _pick(db, 16, (16, 8))
    kc = k_cache.transpose(0, 1, 3, 4, 2).reshape(k_cache.shape[0], db, width, win)
    vc = v_cache.transpose(0, 1, 3, 4, 2).reshape(kc.shape)
    return pl.pallas_call(
        functools.partial(_swa_decode_kernel, bb=bb, kv_heads=kv_heads, group=group, hd=hd),
        grid=(db // bb,),
        in_specs=[
            pl.BlockSpec((bb, heads, width), lambda i: (i, 0, 0)),
            pl.BlockSpec((None, bb, width, win), lambda i: (layer, i, 0, 0)),
            pl.BlockSpec((None, bb, width, win), lambda i: (layer, i, 0, 0)),
            pl.BlockSpec((bb, width), lambda i: (i, 0)),
            pl.BlockSpec((bb, width), lambda i: (i, 0)),
            pl.BlockSpec((heads, 1), lambda i: (0, 0)),
        ],
        out_specs=pl.BlockSpec((bb, heads, hd), lambda i: (i, 0, 0)),
        out_shape=jax.ShapeDtypeStruct((db, heads, hd), BF16),
        compiler_params=_params("parallel"),
        name="swa_decode",
    )(qbd, kc, vc, k_new, v_new, sinks.reshape(heads, 1))


def _mla_attn_kernel(q_ref, k_ref, lat_ref, wuv_ref, o_ref, m_ref, l_ref, acc_ref, *, tq):
    qi = pl.program_id(2)
    m_ref[...] = jnp.full(m_ref.shape, NEG_INF, F32)
    l_ref[...] = jnp.zeros(l_ref.shape, F32)
    acc_ref[...] = jnp.zeros(acc_ref.shape, F32)
    causal = (lax.broadcasted_iota(jnp.int32, (tq, tq), 0) >= lax.broadcasted_iota(jnp.int32, (tq, tq), 1))
    q = q_ref[...]

    def step(ki, masked):
        ks = pl.multiple_of(ki * tq, tq)
        s = _qk(q, k_ref[pl.ds(ks, tq), :])
        if masked:
            s = jnp.where(causal, s, NEG_INF)
        _online_update(s, lat_ref[pl.ds(ks, tq), :], m_ref, l_ref, acc_ref, 0)

    def body(ki, carry):
        step(ki, False)
        return carry

    lax.fori_loop(0, qi, body, 0)
    step(qi, True)
    o_lat = (acc_ref[0] / l_ref[0]).astype(BF16)
    o_ref[...] = jnp.dot(o_lat, wuv_ref[...], preferred_element_type=F32).astype(BF16)


def _mla_attn_prompt(qcat, kcat, latb, w_uv, b, s, heads):
    tq = _pick(s, 512, (128,))
    nq = s // tq
    c = latb.shape[1]
    vd = w_uv.shape[1] // heads
    return pl.pallas_call(
        functools.partial(_mla_attn_kernel, tq=tq),
        grid=(b, heads, nq),
        in_specs=[
            pl.BlockSpec((tq, 2 * LANES), lambda bi, h, qi: (bi * nq + qi, h)),
            pl.BlockSpec((s, 2 * LANES), lambda bi, h, qi: (bi, h)),
            pl.BlockSpec((s, c), lambda bi, h, qi: (bi, 0)),
            pl.BlockSpec((c, vd), lambda bi, h, qi: (0, h)),
        ],
        out_specs=pl.BlockSpec((tq, vd), lambda bi, h, qi: (bi * nq + qi, h)),
        out_shape=jax.ShapeDtypeStruct((b * s, heads * vd), BF16),
        scratch_shapes=[
            pltpu.VMEM((1, tq, 1), F32),
            pltpu.VMEM((1, tq, 1), F32),
            pltpu.VMEM((1, tq, c), F32),
        ],
        compiler_params=_params("parallel", "parallel", "arbitrary"),
        name="mla_attn_prompt",
    )(qcat, kcat, latb, w_uv)


def _mla_decode_kernel(pt_ref, q_ref, *refs, pages, heads, nope, rope):
    lat_pages = refs[:pages]
    kr_pages = refs[pages:2 * pages]
    (ln_ref, krn_ref, wukt_ref, kng_ref, wuv_ref, o_ref,
     u_ref, m_ref, l_ref, acc_ref, ss_ref) = refs[2 * pages:]
    i = pl.program_id(1)

    @pl.when(i == 0)
    def _():
        m_ref[...] = jnp.full(m_ref.shape, NEG_INF, F32)
        l_ref[...] = jnp.zeros(l_ref.shape, F32)
        acc_ref[...] = jnp.zeros(acc_ref.shape, F32)
        qg = (q_ref[:, 0:nope].astype(F32) * kng_ref[...]).astype(BF16)
        wide = jnp.concatenate([qg] * heads, axis=1)
        own = (lax.broadcasted_iota(jnp.int32, wide.shape, 1) // nope
               == lax.broadcasted_iota(jnp.int32, wide.shape, 0))
        qbd = jnp.where(own, wide, jnp.zeros_like(wide))
        u_ref[...] = jnp.dot(qbd, wukt_ref[...], preferred_element_type=F32).astype(BF16)

    q_rope = q_ref[:, nope:nope + rope]

    def attend(lat, s_rope, valid):
        n = lat.shape[0]
        yt = _qk(wukt_ref[...], lat)
        for h in range(heads):
            yh = yt[h * nope:(h + 1) * nope]
            ss_ref[h:h + 1, 0:n] = jnp.sum(yh * yh, axis=0, keepdims=True)
        inv = lax.rsqrt(ss_ref[:, 0:n] * (1.0 / nope) + NORM_EPS)
        s = _qk(u_ref[...], lat) * inv + s_rope
        if valid is not None:
            s = jnp.where(lax.broadcasted_iota(jnp.int32, s.shape, 1) < valid, s, NEG_INF)
        _online_update(s, lat, m_ref, l_ref, acc_ref, 0)

    lat = jnp.concatenate([r[...].astype(BF16) for r in lat_pages], axis=0)
    kr_t = jnp.concatenate([r[...].astype(BF16) for r in kr_pages], axis=1)
    attend(lat, jnp.dot(q_rope, kr_t, preferred_element_type=F32), None)

    @pl.when(i == pl.num_programs(1) - 1)
    def _():
        first_l = lax.broadcasted_iota(jnp.int32, (LANES, ln_ref.shape[1]), 0) == 0
        first_r = lax.broadcasted_iota(jnp.int32, (LANES, rope), 0) == 0
        kr_new = jnp.where(first_r, krn_ref[...], 0.0).astype(BF16)
        attend(jnp.where(first_l, ln_ref[...], 0.0).astype(BF16), _qk(q_rope, kr_new), 1)
        o_lat = (acc_ref[0] / l_ref[0]).astype(BF16)
        full = jnp.dot(o_lat, wuv_ref[...], preferred_element_type=F32)
        vd = full.shape[1] // heads
        own = (lax.broadcasted_iota(jnp.int32, full.shape, 1) // vd
               == lax.broadcasted_iota(jnp.int32, full.shape, 0))
        o_ref[...] = jnp.sum(jnp.where(own, full, 0.0), axis=0, keepdims=True).astype(BF16)


def _mla_decode(qcat, lat_cache, kr_cache, layer, page_table, lat_new, kr_new, w_ukt, kn_gain, w_uv,
                heads, nope, rope):
    db = qcat.shape[0]
    n_pages = page_table.shape[1]
    page, c = lat_cache.shape[2], lat_cache.shape[3]
    pages = math.gcd(n_pages, DECODE_PAGES_PER_STEP)
    ovd = w_uv.shape[1]

    def lat_spec(j):
        return pl.BlockSpec((None, None, page, c), lambda b, i, pt: (layer, pt[b * n_pages + i * pages + j], 0, 0))

    kr_cache = jnp.swapaxes(kr_cache, 2, 3)

    def kr_spec(j):
        return pl.BlockSpec((None, None, rope, page), lambda b, i, pt: (layer, pt[b * n_pages + i * pages + j], 0, 0))

    row3 = lambda b, i, pt: (b, 0, 0)
    fixed = lambda b, i, pt: (0, 0)
    grid_spec = pltpu.PrefetchScalarGridSpec(
        num_scalar_prefetch=1,
        grid=(db, n_pages // pages),
        in_specs=(
            [pl.BlockSpec((None, heads, 2 * LANES), row3)]
            + [lat_spec(j) for j in range(pages)] + [kr_spec(j) for j in range(pages)]
            + [pl.BlockSpec((None, 1, c), row3), pl.BlockSpec((None, 1, rope), row3),
               pl.BlockSpec((heads * nope, c), fixed), pl.BlockSpec((1, nope), fixed),
               pl.BlockSpec((c, ovd), fixed)]
        ),
        out_specs=pl.BlockSpec((None, 1, ovd), row3),
        scratch_shapes=[
            pltpu.VMEM((heads, c), BF16),
            pltpu.VMEM((1, heads, 1), F32),
            pltpu.VMEM((1, heads, 1), F32),
            pltpu.VMEM((1, heads, c), F32),
            pltpu.VMEM((heads, pages * page), F32),
        ],
    )
    return pl.pallas_call(
        functools.partial(_mla_decode_kernel, pages=pages, heads=heads, nope=nope, rope=rope),
        grid_spec=grid_spec,
        out_shape=jax.ShapeDtypeStruct((db, 1, ovd), BF16),
        compiler_params=_params("parallel", "arbitrary"),
        name="mla_decode",
    )(page_table.reshape(-1), qcat, *([lat_cache] * pages), *([kr_cache] * pages),
      lat_new.reshape(db, 1, c), kr_new.reshape(db, 1, rope), w_ukt, kn_gain.reshape(1, nope), w_uv)


def _rope_tables(pos, group):
    half = group // 2
    inv_freq = jnp.power(ROPE_THETA, -jnp.arange(half, dtype=F32) / half)
    ang = pos.astype(F32)[:, None] * inv_freq[None, :]
    cos, sin = jnp.cos(ang), jnp.sin(ang)
    rep = LANES // group
    cosf = jnp.tile(jnp.concatenate([cos, cos], axis=1), (1, rep))
    sinf = jnp.tile(jnp.concatenate([-sin, sin], axis=1), (1, rep))
    return cosf, sinf


def kernel(x_prompt, x_sample, cache_diff_k, cache_diff_v, cache_swa_k, cache_swa_v, cache_mla_latent, cache_mla_krope, page_table, norm_ffn_pre, ffn_pre_gate, ffn_pre_up, ffn_pre_down, norm_mix, norm_ffn_post, ffn_post_gate, ffn_post_up, ffn_post_down, diff_w_qkv, diff_q_gain, diff_k_gain, diff_lambda_q1, diff_lambda_k1, diff_lambda_q2, diff_lambda_k2, diff_sub_gain, diff_w_o, swa_w_qkv, swa_q_gain, swa_k_gain, swa_sinks, swa_w_o, mla_w_down, mla_q_a_gain, mla_kv_a_gain, mla_w_uq, mla_qn_gain, mla_qr_gain, mla_kn_gain, mla_kr_gain, mla_w_uk, mla_w_uv, mla_w_o):
    b, s, d = x_prompt.shape
    db, ds = x_sample.shape[0], x_sample.shape[1]
    assert ds == 1, "one new token per sampled request"
    bs = b * s
    depth = norm_ffn_pre.shape[0]
    past_len = page_table.shape[1] * cache_diff_k.shape[2]

    diff_kv = cache_diff_k.shape[3]
    diff_heads = diff_w_o.shape[1] // (2 * LANES)
    diff_group = diff_heads // diff_kv
    diff_nq, diff_nk = diff_heads * 2 * LANES, diff_kv * 2 * LANES
    assert cache_diff_k.shape[4] == 2 * LANES

    swa_kv, swa_hd = cache_swa_k.shape[3], cache_swa_k.shape[4]
    swa_heads = swa_sinks.shape[1]
    swa_group = swa_heads // swa_kv
    swa_nq, swa_nk = swa_heads * swa_hd, swa_kv * swa_hd
    assert cache_swa_k.shape[2] == WINDOW and s % WINDOW == 0 and LANES % swa_hd == 0

    mla_q_lora, mla_kv_lora = mla_w_uq.shape[1], mla_w_uk.shape[1]
    mla_heads, mla_nope = mla_w_uk.shape[2], mla_w_uk.shape[3]
    mla_rope, mla_vd = cache_mla_krope.shape[3], mla_w_uv.shape[3]
    assert mla_nope == LANES and LANES % mla_rope == 0 and mla_rope < LANES

    x = jnp.concatenate([x_prompt.reshape(bs, d), x_sample.reshape(db, d)], axis=0)
    pos = jnp.concatenate([jnp.tile(jnp.arange(s, dtype=jnp.int32), b),
                           jnp.full((db,), past_len, jnp.int32)])
    cos128, sin128 = _rope_tables(pos, LANES)
    cos64, sin64 = _rope_tables(pos, swa_hd)
    if mla_rope == swa_hd:
        cos_mr, sin_mr = cos64, sin64
    else:
        cos_mr, sin_mr = _rope_tables(pos, mla_rope)

    def ffn(x, gain, wg, wu, wd):
        return _ffn(x, gain, wg.astype(BF16), wu.astype(BF16), wd.astype(BF16))

    outs = {name: [] for name in ("dk_p", "dv_p", "dk_s", "dv_s", "wk_p", "wv_p", "wk_s", "wv_s",
                                  "ml_p", "mr_p", "ml_s", "mr_s")}
    for i in range(depth):
        kind, j = i % 3, i // 3
        x = ffn(x, norm_ffn_pre[i], ffn_pre_gate[i], ffn_pre_up[i], ffn_pre_down[i])
        if kind == 0:
            lam_init = 0.8 - 0.6 * math.exp(-0.3 * i)
            qkv = _norm_mm(x, norm_mix[i], diff_w_qkv[j].astype(BF16), d)
            q, k, v = _diff_prep(qkv, cos128, sin128, diff_q_gain[j], diff_k_gain[j], diff_nq, diff_nk,
                                 LANES ** -0.5)
            lam_prm = (diff_lambda_q1[j], diff_lambda_k1[j], diff_lambda_q2[j], diff_lambda_k2[j])
            o_p = _diff_attn_prompt(q, k, v, lam_prm, diff_sub_gain[j], b, s, diff_kv, diff_group, lam_init)
            qs = q[bs:].reshape(db, diff_kv, diff_group, 2, 1, LANES)
            sel = jnp.eye(2, dtype=BF16).reshape(1, 1, 1, 2, 2, 1)
            qm = (qs * sel).reshape(db, diff_kv, 2 * diff_group, 2 * LANES)
            o_s = _diff_decode(qm, cache_diff_k, cache_diff_v, j, page_table, k[bs:], v[bs:], lam_prm,
                               diff_sub_gain[j], diff_group, lam_init)
            o = jnp.concatenate([o_p, o_s.reshape(db, diff_nq)], axis=0)
            x = _mm_res(o, diff_w_o[j].astype(BF16), x)
            outs["dk_p"].append(k[:bs].reshape(b, s, diff_kv, 2 * LANES))
            outs["dv_p"].append(v[:bs].reshape(b, s, diff_kv, 2 * LANES))
            outs["dk_s"].append(k[bs:].reshape(db, 1, diff_kv, 2 * LANES))
            outs["dv_s"].append(v[bs:].reshape(db, 1, diff_kv, 2 * LANES))
        elif kind == 1:
            qkv = _norm_mm(x, norm_mix[i], swa_w_qkv[j].astype(BF16), d)
            q, k, v = _swa_prep(qkv, cos64, sin64, swa_q_gain[j], swa_k_gain[j], swa_nq, swa_nk, swa_hd,
                                swa_hd ** -0.5)
            o_p = _swa_attn_prompt(q, k, v, swa_sinks[j], b, s, swa_kv, swa_group, swa_hd)
            qs = q[bs:].reshape(db, swa_kv, swa_group, 1, swa_hd)
            sel = jnp.eye(swa_kv, dtype=BF16).reshape(1, swa_kv, 1, swa_kv, 1)
            qbd = (qs * sel).reshape(db, swa_heads, swa_nk)
            o_s = _swa_decode(qbd, cache_swa_k, cache_swa_v, j, k[bs:], v[bs:], swa_sinks[j],
                              swa_kv, swa_group, swa_hd)
            o = jnp.concatenate([o_p, o_s.reshape(db, swa_nq)], axis=0)
            x = _mm_res(o, swa_w_o[j].astype(BF16), x)
            w_keep = min(WINDOW, s)
            k_p = k[:bs].reshape(b, s, swa_kv, swa_hd)
            v_p = v[:bs].reshape(b, s, swa_kv, swa_hd)
            outs["wk_p"].append(k_p[:, s - w_keep:])
            outs["wv_p"].append(v_p[:, s - w_keep:])
            outs["wk_s"].append(jnp.concatenate([cache_swa_k[j][:, 1:], k[bs:].reshape(db, 1, swa_kv, swa_hd)], axis=1))
            outs["wv_s"].append(jnp.concatenate([cache_swa_v[j][:, 1:], v[bs:].reshape(db, 1, swa_kv, swa_hd)], axis=1))
        else:
            down_n = mla_q_lora + mla_kv_lora + mla_rope
            w_down = jnp.pad(mla_w_down[j], ((0, 0), (0, LANES - mla_rope))).astype(BF16)
            dproj = _norm_mm(x, norm_mix[i], w_down, d)
            assert dproj.shape[1] == down_n + LANES - mla_rope
            lat, latb, kr_pad = _mla_down_prep(dproj, cos_mr, sin_mr, mla_kv_a_gain[j], mla_kr_gain[j],
                                               mla_q_lora, mla_kv_lora, mla_rope)
            w_uq = mla_w_uq[j].reshape(mla_q_lora, mla_heads, mla_nope + mla_rope)
            w_uq = jnp.pad(w_uq, ((0, 0), (0, 0), (0, 2 * LANES - mla_nope - mla_rope)))
            w_uq = w_uq.reshape(mla_q_lora, mla_heads * 2 * LANES).astype(BF16)
            qraw = _norm_mm(dproj, mla_q_a_gain[j], w_uq, mla_q_lora)
            qcat = _mla_q_prep(qraw, cos_mr, sin_mr, mla_qn_gain[j], mla_qr_gain[j], mla_heads, mla_rope,
                               (mla_nope + mla_rope) ** -0.5)
            w_uk = mla_w_uk[j].reshape(mla_kv_lora, mla_heads * mla_nope).astype(BF16)
            w_uv = mla_w_uv[j].reshape(mla_kv_lora, mla_heads * mla_vd).astype(BF16)
            kcat = _mla_k_prep(latb, kr_pad, w_uk, mla_kn_gain[j], mla_heads)
            o_p = _mla_attn_prompt(qcat, kcat, latb, w_uv, b, s, mla_heads)
            kr = kr_pad[:, :mla_rope]
            o_s = _mla_decode(qcat[bs:].reshape(db, mla_heads, 2 * LANES), cache_mla_latent, cache_mla_krope, j,
                              page_table, lat[bs:], kr[bs:], w_uk.T, mla_kn_gain[j], w_uv,
                              mla_heads, mla_nope, mla_rope)
            o = jnp.concatenate([o_p, o_s.reshape(db, mla_heads * mla_vd)], axis=0)
            x = _mm_res(o, mla_w_o[j].astype(BF16), x)
            outs["ml_p"].append(lat[:bs].reshape(b, s, mla_kv_lora))
            outs["mr_p"].append(kr[:bs].reshape(b, s, mla_rope))
            outs["ml_s"].append(lat[bs:].reshape(db, 1, mla_kv_lora))
            outs["mr_s"].append(kr[bs:].reshape(db, 1, mla_rope))
        x = ffn(x, norm_ffn_post[i], ffn_post_gate[i], ffn_post_up[i], ffn_post_down[i])

    return (x[:bs].reshape(b, s, d), x[bs:].reshape(db, 1, d),
            *[jnp.stack(outs[name]) for name in ("dk_p", "dv_p", "dk_s", "dv_s", "wk_p", "wv_p", "wk_s", "wv_s",
                                                 "ml_p", "mr_p", "ml_s", "mr_s")])
```

```python
import functools
import math

import jax
import jax.numpy as jnp
from jax import lax
from jax.experimental import pallas as pl
from jax.experimental.pallas import tpu as pltpu

NORM_EPS = 1e-6
ROPE_THETA = 10000.0
WINDOW = 128
LANES = 128
VMEM_LIMIT_BYTES = 56 * 1024 * 1024
DECODE_PAGES_PER_STEP = 16
MLA_DECODE_CHUNK_PAGES = 4
BF16 = jnp.bfloat16
F32 = jnp.float32
NEG_INF = float("-inf")


def _pick(n, cap, mults=(128, 16, 8)):
    for mult in mults:
        best = 0
        for d in range(mult, min(n, cap) + 1, mult):
            if n % d == 0:
                best = d
        if best:
            return best
    return n


def _params(*sem):
    return pltpu.CompilerParams(dimension_semantics=sem, vmem_limit_bytes=VMEM_LIMIT_BYTES)


def _rms(x, gain):
    ms = jnp.mean(x * x, axis=-1, keepdims=True)
    return x * lax.rsqrt(ms + NORM_EPS) * gain


def _ffn_kernel(x_ref, g_ref, wg_ref, wu_ref, wd_ref, o_ref, h_ref):
    @pl.when(pl.program_id(1) == 0)
    def _():
        x = x_ref[...]
        h_ref[...] = _rms(x, g_ref[...]).astype(BF16)
        o_ref[...] = x

    h = h_ref[...]
    g = jnp.dot(h, wg_ref[...].astype(BF16), preferred_element_type=F32)
    u = jnp.dot(h, wu_ref[...].astype(BF16), preferred_element_type=F32)
    act = g * (0.5 / (1.0 + jnp.exp(-g))) * u
    o_ref[...] += jnp.dot(act.astype(BF16), wd_ref[...].astype(BF16), preferred_element_type=F32)


def _ffn(x, gain, wg, wu, wd, layer):
    t, d = x.shape
    f = wg.shape[2]
    tm = _pick(t, 768)
    tf = _pick(f, 256)
    return pl.pallas_call(
        _ffn_kernel,
        grid=(t // tm, f // tf),
        in_specs=[
            pl.BlockSpec((tm, d), lambda i, j: (i, 0)),
            pl.BlockSpec((1, d), lambda i, j: (0, 0)),
            pl.BlockSpec((None, d, tf), lambda i, j: (layer, 0, j)),
            pl.BlockSpec((None, d, tf), lambda i, j: (layer, 0, j)),
            pl.BlockSpec((None, tf, d), lambda i, j: (layer, j, 0)),
        ],
        out_specs=pl.BlockSpec((tm, d), lambda i, j: (i, 0)),
        out_shape=jax.ShapeDtypeStruct((t, d), F32),
        scratch_shapes=[pltpu.VMEM((tm, d), BF16)],
        compiler_params=_params("parallel", "arbitrary"),
        name="ffn",
    )(x, gain.reshape(1, d), wg, wu, wd)


def _norm_mm_kernel(x_ref, g_ref, w_ref, o_ref, h_ref):
    @pl.when(pl.program_id(1) == 0)
    def _():
        h_ref[...] = _rms(x_ref[...], g_ref[...]).astype(BF16)

    o_ref[...] = jnp.dot(h_ref[...], w_ref[...], preferred_element_type=F32)


def _norm_mm(x, gain, w, k):
    t = x.shape[0]
    n = w.shape[1]
    tm = _pick(t, 768)
    tn = _pick(n, 1024)
    return pl.pallas_call(
        _norm_mm_kernel,
        grid=(t // tm, n // tn),
        in_specs=[
            pl.BlockSpec((tm, k), lambda i, j: (i, 0)),
            pl.BlockSpec((1, k), lambda i, j: (0, 0)),
            pl.BlockSpec((k, tn), lambda i, j: (0, j)),
        ],
        out_specs=pl.BlockSpec((tm, tn), lambda i, j: (i, j)),
        out_shape=jax.ShapeDtypeStruct((t, n), F32),
        scratch_shapes=[pltpu.VMEM((tm, k), BF16)],
        compiler_params=_params("parallel", "arbitrary"),
        name="norm_mm",
    )(x, gain.reshape(1, k), w)


def _mm_res_kernel(a_ref, w_ref, x_ref, o_ref):
    o_ref[...] = x_ref[...] + jnp.dot(a_ref[...], w_ref[...], preferred_element_type=F32)


def _mm_res(a, w, x):
    t, k = a.shape
    n = w.shape[1]
    tm = _pick(t, 768)
    tn = _pick(n, 1024)
    return pl.pallas_call(
        _mm_res_kernel,
        grid=(t // tm, n // tn),
        in_specs=[
            pl.BlockSpec((tm, k), lambda i, j: (i, 0)),
            pl.BlockSpec((k, tn), lambda i, j: (0, j)),
            pl.BlockSpec((tm, tn), lambda i, j: (i, j)),
        ],
        out_specs=pl.BlockSpec((tm, tn), lambda i, j: (i, j)),
        out_shape=jax.ShapeDtypeStruct((t, n), F32),
        compiler_params=_params("parallel", "parallel"),
        name="mm_res",
    )(a, w, x)


def _head_norm(x, gain, group):
    xx = x * x
    if group == LANES:
        ms = jnp.mean(xx, axis=-1, keepdims=True)
    else:
        lo = lax.broadcasted_iota(jnp.int32, x.shape, 1) < group
        s_lo = jnp.sum(jnp.where(lo, xx, 0.0), axis=-1, keepdims=True)
        s_hi = jnp.sum(jnp.where(lo, 0.0, xx), axis=-1, keepdims=True)
        ms = jnp.where(lo, s_lo, s_hi) * (1.0 / group)
    return x * lax.rsqrt(ms + NORM_EPS) * gain


def _rope(y, cosf, sinf, group):
    if group == LANES:
        rot = pltpu.roll(y, LANES // 2, 1)
    else:
        lane = lax.broadcasted_iota(jnp.int32, y.shape, 1)
        first_half = (lane % group) < (group // 2)
        rot = jnp.where(first_half, pltpu.roll(y, LANES - group // 2, 1), pltpu.roll(y, group // 2, 1))
    return y * cosf + rot * sinf


def _diff_prep_kernel(qkv_ref, cos_ref, sin_ref, qg_ref, kg_ref, q_ref, k_ref, v_ref, *, nq, nk, scale):
    cosf, sinf = cos_ref[...], sin_ref[...]
    for c in range(nq):
        y = _head_norm(qkv_ref[:, c * LANES:(c + 1) * LANES], qg_ref[...], LANES)
        q_ref[:, c * LANES:(c + 1) * LANES] = (_rope(y, cosf, sinf, LANES) * scale).astype(BF16)
    for c in range(nk):
        y = _head_norm(qkv_ref[:, (nq + c) * LANES:(nq + c + 1) * LANES], kg_ref[...], LANES)
        k_ref[:, c * LANES:(c + 1) * LANES] = _rope(y, cosf, sinf, LANES)
    v_ref[...] = qkv_ref[:, (nq + nk) * LANES:]


def _diff_prep(qkv, cosf, sinf, q_gain, k_gain, nq_cols, nk_cols, scale):
    t, n = qkv.shape
    tm = _pick(t, 768)
    nv_cols = n - nq_cols - nk_cols
    row = lambda i: (i, 0)
    fixed = lambda i: (0, 0)
    return pl.pallas_call(
        functools.partial(_diff_prep_kernel, nq=nq_cols // LANES, nk=nk_cols // LANES, scale=scale),
        grid=(t // tm,),
        in_specs=[
            pl.BlockSpec((tm, n), row),
            pl.BlockSpec((tm, LANES), row),
            pl.BlockSpec((tm, LANES), row),
            pl.BlockSpec((1, LANES), fixed),
            pl.BlockSpec((1, LANES), fixed),
        ],
        out_specs=[
            pl.BlockSpec((tm, nq_cols), row),
            pl.BlockSpec((tm, nk_cols), row),
            pl.BlockSpec((tm, nv_cols), row),
        ],
        out_shape=[
            jax.ShapeDtypeStruct((t, nq_cols), BF16),
            jax.ShapeDtypeStruct((t, nk_cols), F32),
            jax.ShapeDtypeStruct((t, nv_cols), F32),
        ],
        compiler_params=_params("parallel"),
        name="diff_prep",
    )(qkv, cosf, sinf, q_gain.reshape(1, LANES), k_gain.reshape(1, LANES))


def _swa_prep_kernel(qkv_ref, cos_ref, sin_ref, qg_ref, kg_ref, q_ref, k_ref, v_ref, *, nq, nk, group, scale):
    cosf, sinf = cos_ref[...], sin_ref[...]
    for c in range(nq):
        y = _head_norm(qkv_ref[:, c * LANES:(c + 1) * LANES], qg_ref[...], group)
        q_ref[:, c * LANES:(c + 1) * LANES] = (_rope(y, cosf, sinf, group) * scale).astype(BF16)
    for c in range(nk):
        y = _head_norm(qkv_ref[:, (nq + c) * LANES:(nq + c + 1) * LANES], kg_ref[...], group)
        k_ref[:, c * LANES:(c + 1) * LANES] = _rope(y, cosf, sinf, group)
    v_ref[...] = qkv_ref[:, (nq + nk) * LANES:]


def _swa_prep(qkv, cosf, sinf, q_gain, k_gain, nq_cols, nk_cols, group, scale):
    t, n = qkv.shape
    tm = _pick(t, 768)
    nv_cols = n - nq_cols - nk_cols
    rep = LANES // group
    row = lambda i: (i, 0)
    fixed = lambda i: (0, 0)
    return pl.pallas_call(
        functools.partial(_swa_prep_kernel, nq=nq_cols // LANES, nk=nk_cols // LANES, group=group, scale=scale),
        grid=(t // tm,),
        in_specs=[
            pl.BlockSpec((tm, n), row),
            pl.BlockSpec((tm, LANES), row),
            pl.BlockSpec((tm, LANES), row),
            pl.BlockSpec((1, LANES), fixed),
            pl.BlockSpec((1, LANES), fixed),
        ],
        out_specs=[
            pl.BlockSpec((tm, nq_cols), row),
            pl.BlockSpec((tm, nk_cols), row),
            pl.BlockSpec((tm, nv_cols), row),
        ],
        out_shape=[
            jax.ShapeDtypeStruct((t, nq_cols), BF16),
            jax.ShapeDtypeStruct((t, nk_cols), F32),
            jax.ShapeDtypeStruct((t, nv_cols), F32),
        ],
        compiler_params=_params("parallel"),
        name="swa_prep",
    )(qkv, cosf, sinf, jnp.tile(q_gain, rep).reshape(1, LANES), jnp.tile(k_gain, rep).reshape(1, LANES))


def _mla_down_prep_kernel(d_ref, cos_ref, sin_ref, kvg_ref, krg_ref, lat_ref, latb_ref, kr_ref, *, q_lora, kv_lora, rope):
    lat = _rms(d_ref[:, q_lora:q_lora + kv_lora], kvg_ref[...])
    lat_ref[...] = lat
    latb_ref[...] = lat.astype(BF16)
    y = _head_norm(d_ref[:, q_lora + kv_lora:q_lora + kv_lora + LANES], krg_ref[...], rope)
    kr_ref[...] = _rope(y, cos_ref[...], sin_ref[...], rope)


def _mla_down_prep(d, cosf, sinf, kv_gain, kr_gain, q_lora, kv_lora, rope):
    t, n = d.shape
    tm = _pick(t, 768)
    row = lambda i: (i, 0)
    fixed = lambda i: (0, 0)
    krg = jnp.concatenate([kr_gain, jnp.zeros((LANES - rope,), F32)]).reshape(1, LANES)
    return pl.pallas_call(
        functools.partial(_mla_down_prep_kernel, q_lora=q_lora, kv_lora=kv_lora, rope=rope),
        grid=(t // tm,),
        in_specs=[
            pl.BlockSpec((tm, n), row),
            pl.BlockSpec((tm, LANES), row),
            pl.BlockSpec((tm, LANES), row),
            pl.BlockSpec((1, kv_lora), fixed),
            pl.BlockSpec((1, LANES), fixed),
        ],
        out_specs=[
            pl.BlockSpec((tm, kv_lora), row),
            pl.BlockSpec((tm, kv_lora), row),
            pl.BlockSpec((tm, LANES), row),
        ],
        out_shape=[
            jax.ShapeDtypeStruct((t, kv_lora), F32),
            jax.ShapeDtypeStruct((t, kv_lora), BF16),
            jax.ShapeDtypeStruct((t, LANES), F32),
        ],
        compiler_params=_params("parallel"),
        name="mla_down_prep",
    )(d, cosf, sinf, kv_gain.reshape(1, kv_lora), krg)


def _mla_q_prep_kernel(q_ref, cos_ref, sin_ref, ng_ref, rg_ref, o_ref, *, heads, rope, scale):
    cosf, sinf = cos_ref[...], sin_ref[...]
    for h in range(heads):
        lo = 2 * h * LANES
        y = _head_norm(q_ref[:, lo:lo + LANES], ng_ref[...], LANES)
        o_ref[:, lo:lo + LANES] = (y * scale).astype(BF16)
        y = _head_norm(q_ref[:, lo + LANES:lo + 2 * LANES], rg_ref[...], rope)
        o_ref[:, lo + LANES:lo + 2 * LANES] = (_rope(y, cosf, sinf, rope) * scale).astype(BF16)


def _mla_q_prep(q, cosf, sinf, qn_gain, qr_gain, heads, rope, scale):
    t, n = q.shape
    tm = _pick(t, 768)
    row = lambda i: (i, 0)
    fixed = lambda i: (0, 0)
    rg = jnp.concatenate([qr_gain, jnp.zeros((LANES - rope,), F32)]).reshape(1, LANES)
    return pl.pallas_call(
        functools.partial(_mla_q_prep_kernel, heads=heads, rope=rope, scale=scale),
        grid=(t // tm,),
        in_specs=[
            pl.BlockSpec((tm, n), row),
            pl.BlockSpec((tm, LANES), row),
            pl.BlockSpec((tm, LANES), row),
            pl.BlockSpec((1, LANES), fixed),
            pl.BlockSpec((1, LANES), fixed),
        ],
        out_specs=pl.BlockSpec((tm, n), row),
        out_shape=jax.ShapeDtypeStruct((t, n), BF16),
        compiler_params=_params("parallel"),
        name="mla_q_prep",
    )(q, cosf, sinf, qn_gain.reshape(1, LANES), rg)


def _mla_k_prep_kernel(lat_ref, kr_ref, w_ref, g_ref, o_ref, *, heads):
    y = jnp.dot(lat_ref[...], w_ref[...], preferred_element_type=F32)
    kr = kr_ref[...].astype(BF16)
    for h in range(heads):
        o_ref[:, 2 * h * LANES:(2 * h + 1) * LANES] = _rms(y[:, h * LANES:(h + 1) * LANES], g_ref[...]).astype(BF16)
        o_ref[:, (2 * h + 1) * LANES:(2 * h + 2) * LANES] = kr


def _mla_k_prep(latb, kr_pad, w_uk, kn_gain, heads):
    t, c = latb.shape
    tm = _pick(t, 768)
    row = lambda i: (i, 0)
    fixed = lambda i: (0, 0)
    return pl.pallas_call(
        functools.partial(_mla_k_prep_kernel, heads=heads),
        grid=(t // tm,),
        in_specs=[
            pl.BlockSpec((tm, c), row),
            pl.BlockSpec((tm, LANES), row),
            pl.BlockSpec((c, heads * LANES), fixed),
            pl.BlockSpec((1, LANES), fixed),
        ],
        out_specs=pl.BlockSpec((tm, 2 * heads * LANES), row),
        out_shape=jax.ShapeDtypeStruct((t, 2 * heads * LANES), BF16),
        compiler_params=_params("parallel"),
        name="mla_k_prep",
    )(latb, kr_pad, w_uk, kn_gain.reshape(1, LANES))


def _lanes(col, width=LANES):
    return jnp.broadcast_to(col, (col.shape[0], width))


def _wide(rep, width):
    return rep if width == LANES else jnp.tile(rep, (1, width // LANES))


def _online_update(s, pv, m_ref, l_ref, acc_ref, idx):
    keys = s.shape[1]
    m_prev = m_ref[idx]
    m_new = jnp.maximum(m_prev, _lanes(jnp.max(s, axis=-1, keepdims=True)))
    alpha = jnp.exp(m_prev - m_new)
    p = jnp.exp(s - _wide(m_new, keys))
    l_ref[idx] = alpha * l_ref[idx] + _lanes(jnp.sum(p, axis=-1, keepdims=True))
    acc_ref[idx] = _wide(alpha, acc_ref.shape[-1]) * acc_ref[idx] + pv(p)
    m_ref[idx] = m_new


def _pv(v):
    return lambda p: jnp.dot(p.astype(BF16), v, preferred_element_type=F32)


def _qk(q, k):
    return lax.dot_general(q, k, (((1,), (1,)), ((), ())), preferred_element_type=F32)


def _diff_lambda(lq1, lk1, lq2, lk2, lam_init):
    return (jnp.exp(jnp.sum(lq1[...] * lk1[...], axis=-1, keepdims=True))
            - jnp.exp(jnp.sum(lq2[...] * lk2[...], axis=-1, keepdims=True)) + lam_init)


def _diff_attn_kernel(q_ref, k_ref, v_ref, lq1, lk1, lq2, lk2, sg_ref, o_ref, m_ref, l_ref, acc_ref,
                      *, tq, group, lam_init):
    qi = pl.program_id(2)
    m_ref[...] = jnp.full(m_ref.shape, NEG_INF, F32)
    l_ref[...] = jnp.zeros(l_ref.shape, F32)
    acc_ref[...] = jnp.zeros(acc_ref.shape, F32)
    hd = LANES
    causal = (lax.broadcasted_iota(jnp.int32, (tq, tq), 0) >= lax.broadcasted_iota(jnp.int32, (tq, tq), 1))

    def step(ki, masked):
        ks = pl.multiple_of(ki * tq, tq)
        kb = k_ref[pl.ds(ks, tq), :].astype(BF16)
        vb = v_ref[pl.ds(ks, tq), :].astype(BF16)
        for idx in range(2 * group):
            c = idx % 2
            s = _qk(q_ref[:, idx * hd:(idx + 1) * hd], kb[:, c * hd:(c + 1) * hd])
            if masked:
                s = jnp.where(causal, s, NEG_INF)
            _online_update(s, _pv(vb), m_ref, l_ref, acc_ref, idx)

    def body(ki, carry):
        step(ki, False)
        return carry

    lax.fori_loop(0, qi, body, 0)
    step(qi, True)

    lam = _diff_lambda(lq1, lk1, lq2, lk2, lam_init)
    for g in range(group):
        o = (acc_ref[2 * g] / _wide(l_ref[2 * g], 2 * hd)
             - lam * (acc_ref[2 * g + 1] / _wide(l_ref[2 * g + 1], 2 * hd)))
        o = _rms(o, sg_ref[...]) * (1.0 - lam_init)
        o_ref[:, g * 2 * hd:(g + 1) * 2 * hd] = o.astype(BF16)


def _diff_attn_prompt(q, k, v, lam_prm, sub_gain, b, s, kv_heads, group, lam_init):
    hd = LANES
    tq = _pick(s, 512, (128,))
    nq = s // tq
    qw = group * 2 * hd
    vec = lambda *_: (0, 0)
    return pl.pallas_call(
        functools.partial(_diff_attn_kernel, tq=tq, group=group, lam_init=lam_init),
        grid=(b, kv_heads, nq),
        in_specs=[
            pl.BlockSpec((tq, qw), lambda bi, h, qi: (bi * nq + qi, h)),
            pl.BlockSpec((s, 2 * hd), lambda bi, h, qi: (bi, h)),
            pl.BlockSpec((s, 2 * hd), lambda bi, h, qi: (bi, h)),
            pl.BlockSpec((1, hd), vec), pl.BlockSpec((1, hd), vec),
            pl.BlockSpec((1, hd), vec), pl.BlockSpec((1, hd), vec),
            pl.BlockSpec((1, 2 * hd), vec),
        ],
        out_specs=pl.BlockSpec((tq, qw), lambda bi, h, qi: (bi * nq + qi, h)),
        out_shape=jax.ShapeDtypeStruct((b * s, kv_heads * qw), BF16),
        scratch_shapes=[
            pltpu.VMEM((2 * group, tq, LANES), F32),
            pltpu.VMEM((2 * group, tq, LANES), F32),
            pltpu.VMEM((2 * group, tq, 2 * hd), F32),
        ],
        compiler_params=_params("parallel", "parallel", "arbitrary"),
        name="diff_attn_prompt",
    )(q, k, v, *[p.reshape(1, hd) for p in lam_prm], sub_gain.reshape(1, 2 * hd))


def _diff_decode_kernel(pt_ref, qm_ref, *refs, pages, page, kv_heads, group, lam_init):
    k_pages = refs[:pages]
    v_pages = refs[pages:2 * pages]
    kn_ref, vn_ref, lq1, lk1, lq2, lk2, sg_ref, o_ref, m_ref, l_ref, acc_ref = refs[2 * pages:]
    i = pl.program_id(1)
    hw = 2 * LANES

    def head_rows(page_refs, h):
        halves = [jnp.concatenate([r[pl.ds(c * kv_heads + h, page, stride=2 * kv_heads), :] for r in page_refs], axis=0)
                  for c in range(2)]
        return jnp.concatenate(halves, axis=1).astype(BF16)

    @pl.when(i == 0)
    def _():
        m_ref[...] = jnp.full(m_ref.shape, NEG_INF, F32)
        l_ref[...] = jnp.zeros(l_ref.shape, F32)
        acc_ref[...] = jnp.zeros(acc_ref.shape, F32)

    gr = 2 * group

    def scores(keys):
        return jnp.concatenate([_qk(qm_ref[h], keys[h]) for h in range(kv_heads)], axis=0)

    def values(vals):
        return lambda p: jnp.concatenate(
            [jnp.dot(p[h * gr:(h + 1) * gr].astype(BF16), vals[h], preferred_element_type=F32)
             for h in range(kv_heads)], axis=0)

    _online_update(scores([head_rows(k_pages, h) for h in range(kv_heads)]),
                   values([head_rows(v_pages, h) for h in range(kv_heads)]), m_ref, l_ref, acc_ref, 0)

    @pl.when(i == pl.num_programs(1) - 1)
    def _():
        first = lax.broadcasted_iota(jnp.int32, (LANES, hw), 0) == 0
        s = scores([jnp.where(first, kn_ref[:, h * hw:(h + 1) * hw], 0.0).astype(BF16) for h in range(kv_heads)])
        s = jnp.where(lax.broadcasted_iota(jnp.int32, s.shape, 1) == 0, s, NEG_INF)
        _online_update(s, values([jnp.where(first, vn_ref[:, h * hw:(h + 1) * hw], 0.0).astype(BF16)
                                  for h in range(kv_heads)]), m_ref, l_ref, acc_ref, 0)
        lam = _diff_lambda(lq1, lk1, lq2, lk2, lam_init)
        a = acc_ref[0] / _wide(l_ref[0], hw)
        for hg in range(kv_heads * group):
            o = a[2 * hg:2 * hg + 1] - lam * a[2 * hg + 1:2 * hg + 2]
            o = _rms(o, sg_ref[...]) * (1.0 - lam_init)
            o_ref[:, hg * hw:(hg + 1) * hw] = o.astype(BF16)


def _diff_decode(qm, k_cache, v_cache, layer, page_table, k_new, v_new, lam_prm, sub_gain, group, lam_init):
    db, kv_heads = qm.shape[0], qm.shape[1]
    n_pages = page_table.shape[1]
    hw = 2 * LANES
    width = kv_heads * hw
    page = k_cache.shape[2]
    pages = math.gcd(n_pages, DECODE_PAGES_PER_STEP)
    layers, pool = k_cache.shape[0], k_cache.shape[1]
    rows = page * 2 * kv_heads

    def page_rows(cache):
        c = cache.reshape(layers, pool, page, kv_heads, 2, LANES)
        return c.transpose(0, 1, 2, 4, 3, 5).reshape(layers, pool, rows, LANES)

    kc, vc = page_rows(k_cache), page_rows(v_cache)

    def page_spec(j):
        return pl.BlockSpec((None, None, rows, LANES),
                            lambda b, i, pt: (layer, pt[b * n_pages + i * pages + j], 0, 0))

    row3 = lambda b, i, pt: (b, 0, 0)
    vec = lambda b, i, pt: (0, 0)
    grid_spec = pltpu.PrefetchScalarGridSpec(
        num_scalar_prefetch=1,
        grid=(db, n_pages // pages),
        in_specs=(
            [pl.BlockSpec((None, kv_heads, 2 * group, hw), lambda b, i, pt: (b, 0, 0, 0))]
            + [page_spec(j) for j in range(pages)] + [page_spec(j) for j in range(pages)]
            + [pl.BlockSpec((None, 1, width), row3), pl.BlockSpec((None, 1, width), row3)]
            + [pl.BlockSpec((1, LANES), vec)] * 4 + [pl.BlockSpec((1, hw), vec)]
        ),
        out_specs=pl.BlockSpec((None, 1, kv_heads * group * hw), row3),
        scratch_shapes=[
            pltpu.VMEM((1, kv_heads * 2 * group, LANES), F32),
            pltpu.VMEM((1, kv_heads * 2 * group, LANES), F32),
            pltpu.VMEM((1, kv_heads * 2 * group, hw), F32),
        ],
    )
    return pl.pallas_call(
        functools.partial(_diff_decode_kernel, pages=pages, page=page, kv_heads=kv_heads, group=group,
                          lam_init=lam_init),
        grid_spec=grid_spec,
        out_shape=jax.ShapeDtypeStruct((db, 1, kv_heads * group * hw), BF16),
        compiler_params=_params("parallel", "arbitrary"),
        name="diff_decode",
    )(page_table.reshape(-1), qm, *([kc] * pages), *([vc] * pages),
      k_new.reshape(db, 1, width), v_new.reshape(db, 1, width),
      *[p.reshape(1, LANES) for p in lam_prm], sub_gain.reshape(1, hw))


def _swa_attn_kernel(q_ref, kp_ref, kc_ref, vp_ref, vc_ref, sink_ref, o_ref, *, kv_heads, group, hd, win):
    blk = pl.program_id(1)
    r = lax.broadcasted_iota(jnp.int32, (win, 2 * win), 0)
    c = lax.broadcasted_iota(jnp.int32, (win, 2 * win), 1)
    dist = win + r - c
    ok = (dist >= 0) & (dist <= win) & ((c >= win) | (blk > 0))
    bias = jnp.tile(jnp.where(ok, 0.0, NEG_INF), (group, 1))
    low_q = lax.broadcasted_iota(jnp.int32, (win, LANES), 1) < hd
    keep = (jnp.where(low_q, 1.0, 0.0).astype(BF16), jnp.where(low_q, 0.0, 1.0).astype(BF16))
    low_k = lax.broadcasted_iota(jnp.int32, (2 * win, LANES), 1) < hd
    kcat = jnp.concatenate([kp_ref[...], kc_ref[...]], axis=0)
    vcat = jnp.concatenate([vp_ref[...], vc_ref[...]], axis=0)

    def both_halves(chunk, half):
        swapped = pltpu.roll(chunk, hd, 1)
        return (jnp.where(low_k, chunk, swapped) if half == 0 else jnp.where(low_k, swapped, chunk)).astype(BF16)

    for j in range(kv_heads):
        t, half = divmod(j, 2)
        kk = both_halves(kcat[:, t * LANES:(t + 1) * LANES], half)
        vv = both_halves(vcat[:, t * LANES:(t + 1) * LANES], half)
        heads = [j * group + g for g in range(group)]
        q = jnp.concatenate([q_ref[:, (h // 2) * LANES:(h // 2 + 1) * LANES] * keep[h % 2] for h in heads], axis=0)
        sink = jnp.concatenate([jnp.broadcast_to(sink_ref[h:h + 1, :], (win, LANES)) for h in heads], axis=0)
        s = _qk(q, kk) + bias
        m = jnp.maximum(_lanes(jnp.max(s, axis=-1, keepdims=True)), sink)
        p = jnp.exp(s - _wide(m, 2 * win))
        den = _lanes(jnp.sum(p, axis=-1, keepdims=True)) + jnp.exp(sink - m)
        o = jnp.dot(p.astype(BF16), vv, preferred_element_type=F32) / den
        for gp in range(group // 2):
            pair = jnp.where(low_q, o[2 * gp * win:(2 * gp + 1) * win], o[(2 * gp + 1) * win:(2 * gp + 2) * win])
            chunk = j * group // 2 + gp
            o_ref[:, chunk * LANES:(chunk + 1) * LANES] = pair.astype(BF16)


def _swa_attn_prompt(q, k, v, sinks, b, s, kv_heads, group, hd):
    win = WINDOW
    nb = s // win
    heads = kv_heads * group
    assert 2 * hd == LANES and group % 2 == 0 and kv_heads % 2 == 0, "two heads per 128-lane chunk"
    cur = lambda bi, i: (bi * nb + i, 0)
    prev = lambda bi, i: (bi * nb + jnp.maximum(i - 1, 0), 0)
    return pl.pallas_call(
        functools.partial(_swa_attn_kernel, kv_heads=kv_heads, group=group, hd=hd, win=win),
        grid=(b, nb),
        in_specs=[
            pl.BlockSpec((win, heads * hd), cur),
            pl.BlockSpec((win, kv_heads * hd), prev),
            pl.BlockSpec((win, kv_heads * hd), cur),
            pl.BlockSpec((win, kv_heads * hd), prev),
            pl.BlockSpec((win, kv_heads * hd), cur),
            pl.BlockSpec((heads, LANES), lambda bi, i: (0, 0)),
        ],
        out_specs=pl.BlockSpec((win, heads * hd), cur),
        out_shape=jax.ShapeDtypeStruct((b * s, heads * hd), BF16),
        compiler_params=_params("parallel", "parallel"),
        name="swa_attn_prompt",
    )(q, k, k, v, v, jnp.broadcast_to(sinks.reshape(heads, 1), (heads, LANES)))


def _swa_decode_kernel(q_ref, k_ref, v_ref, kn_ref, vn_ref, sink_ref, o_ref, *, bb, kv_heads, group, hd):
    heads = kv_heads * group
    width = kv_heads * hd
    own = (lax.broadcasted_iota(jnp.int32, (heads, width), 1) // hd
           == lax.broadcasted_iota(jnp.int32, (heads, width), 0) // group)
    sink = sink_ref[...]
    for bi in range(bb):
        q = q_ref[bi]
        s = jnp.dot(q, k_ref[bi].astype(BF16), preferred_element_type=F32)
        s_new = jnp.sum(q.astype(F32) * kn_ref[bi:bi + 1, :], axis=-1, keepdims=True)
        m = jnp.maximum(jnp.maximum(jnp.max(s, axis=-1, keepdims=True), s_new), sink)
        p = jnp.exp(s - m)
        p_new = jnp.exp(s_new - m)
        den = jnp.sum(p, axis=-1, keepdims=True) + p_new + jnp.exp(sink - m)
        o = _qk(p.astype(BF16), v_ref[bi].astype(BF16)) + p_new * vn_ref[bi:bi + 1, :]
        o = jnp.where(own, o / den, 0.0)
        acc = o[:, 0:hd]
        for j in range(1, kv_heads):
            acc = acc + o[:, j * hd:(j + 1) * hd]
        o_ref[bi] = acc.astype(BF16)


def _swa_decode(qbd, k_cache, v_cache, layer, k_new, v_new, sinks, kv_heads, group, hd):
    db, heads, width = qbd.shape
    win = k_cache.shape[2]
    bb = _pick(db, 16, (16, 8))
    kc = k_cache.transpose(0, 1, 3, 4, 2).reshape(k_cache.shape[0], db, width, win)
    vc = v_cache.transpose(0, 1, 3, 4, 2).reshape(kc.shape)
    return pl.pallas_call(
        functools.partial(_swa_decode_kernel, bb=bb, kv_heads=kv_heads, group=group, hd=hd),
        grid=(db // bb,),
        in_specs=[
            pl.BlockSpec((bb, heads, width), lambda i: (i, 0, 0)),
            pl.BlockSpec((None, bb, width, win), lambda i: (layer, i, 0, 0)),
            pl.BlockSpec((None, bb, width, win), lambda i: (layer, i, 0, 0)),
            pl.BlockSpec((bb, width), lambda i: (i, 0)),
            pl.BlockSpec((bb, width), lambda i: (i, 0)),
            pl.BlockSpec((heads, 1), lambda i: (0, 0)),
        ],
        out_specs=pl.BlockSpec((bb, heads, hd), lambda i: (i, 0, 0)),
        out_shape=jax.ShapeDtypeStruct((db, heads, hd), BF16),
        compiler_params=_params("parallel"),
        name="swa_decode",
    )(qbd, kc, vc, k_new, v_new, sinks.reshape(heads, 1))


def _mla_attn_kernel(q_ref, k_ref, lat_ref, wuv_ref, o_ref, m_ref, l_ref, acc_ref, *, tq, hp, vd):
    qi = pl.program_id(2)
    m_ref[...] = jnp.full(m_ref.shape, NEG_INF, F32)
    l_ref[...] = jnp.zeros(l_ref.shape, F32)
    acc_ref[...] = jnp.zeros(acc_ref.shape, F32)
    causal = (lax.broadcasted_iota(jnp.int32, (tq, tq), 0) >= lax.broadcasted_iota(jnp.int32, (tq, tq), 1))
    kw = 2 * LANES

    def step(ki, masked):
        ks = pl.multiple_of(ki * tq, tq)
        lat = lat_ref[pl.ds(ks, tq), :]
        for h in range(hp):
            s = _qk(q_ref[:, h * kw:(h + 1) * kw], k_ref[pl.ds(ks, tq), h * kw:(h + 1) * kw])
            if masked:
                s = jnp.where(causal, s, NEG_INF)
            _online_update(s, _pv(lat), m_ref, l_ref, acc_ref, h)

    def body(ki, carry):
        step(ki, False)
        return carry

    lax.fori_loop(0, qi, body, 0)
    step(qi, True)
    for h in range(hp):
        o_lat = (acc_ref[h] / _wide(l_ref[h], acc_ref.shape[-1])).astype(BF16)
        o_ref[:, h * vd:(h + 1) * vd] = jnp.dot(o_lat, wuv_ref[:, h * vd:(h + 1) * vd],
                                                 preferred_element_type=F32).astype(BF16)


def _mla_attn_prompt(qcat, kcat, latb, w_uv, b, s, heads):
    tq = _pick(s, 512, (128,))
    nq = s // tq
    c = latb.shape[1]
    vd = w_uv.shape[1] // heads
    hp = math.gcd(heads, 4)
    return pl.pallas_call(
        functools.partial(_mla_attn_kernel, tq=tq, hp=hp, vd=vd),
        grid=(b, heads // hp, nq),
        in_specs=[
            pl.BlockSpec((tq, hp * 2 * LANES), lambda bi, h, qi: (bi * nq + qi, h)),
            pl.BlockSpec((s, hp * 2 * LANES), lambda bi, h, qi: (bi, h)),
            pl.BlockSpec((s, c), lambda bi, h, qi: (bi, 0)),
            pl.BlockSpec((c, hp * vd), lambda bi, h, qi: (0, h)),
        ],
        out_specs=pl.BlockSpec((tq, hp * vd), lambda bi, h, qi: (bi * nq + qi, h)),
        out_shape=jax.ShapeDtypeStruct((b * s, heads * vd), BF16),
        scratch_shapes=[
            pltpu.VMEM((hp, tq, LANES), F32),
            pltpu.VMEM((hp, tq, LANES), F32),
            pltpu.VMEM((hp, tq, c), F32),
        ],
        compiler_params=_params("parallel", "parallel", "arbitrary"),
        name="mla_attn_prompt",
    )(qcat, kcat, latb, w_uv)


def _mla_decode_kernel(pt_ref, q_ref, *refs, pages, chunk, heads, nope, rope):
    lat_pages = refs[:pages]
    kr_pages = refs[pages:2 * pages]
    (ln_ref, krn_ref, wukt_ref, kng_ref, wuv_ref, o_ref,
     u_ref, m_ref, l_ref, acc_ref, ss_ref) = refs[2 * pages:]
    i = pl.program_id(1)

    @pl.when(i == 0)
    def _():
        m_ref[...] = jnp.full(m_ref.shape, NEG_INF, F32)
        l_ref[...] = jnp.zeros(l_ref.shape, F32)
        acc_ref[...] = jnp.zeros(acc_ref.shape, F32)
        qg = (q_ref[:, 0:nope].astype(F32) * kng_ref[...]).astype(BF16)
        wide = jnp.concatenate([qg] * heads, axis=1)
        own = (lax.broadcasted_iota(jnp.int32, wide.shape, 1) // nope
               == lax.broadcasted_iota(jnp.int32, wide.shape, 0))
        qbd = jnp.where(own, wide, jnp.zeros_like(wide))
        u_ref[...] = jnp.dot(qbd, wukt_ref[...], preferred_element_type=F32).astype(BF16)

    q_rope = q_ref[:, nope:nope + rope]

    def key_sumsq(lat, col):
        n = lat.shape[0]
        yt = _qk(wukt_ref[...], lat)
        for h in range(heads):
            yh = yt[h * nope:(h + 1) * nope]
            ss_ref[h:h + 1, col:col + n] = jnp.sum(yh * yh, axis=0, keepdims=True)

    def attend(lat, s_rope, valid):
        n = lat.shape[0]
        inv = lax.rsqrt(ss_ref[:, 0:n] * (1.0 / nope) + NORM_EPS)
        s = _qk(u_ref[...], lat) * inv + s_rope
        if valid is not None:
            s = jnp.where(lax.broadcasted_iota(jnp.int32, s.shape, 1) < valid, s, NEG_INF)
        _online_update(s, _pv(lat), m_ref, l_ref, acc_ref, 0)

    chunks = []
    for c0 in range(0, pages, chunk):
        lat_c = jnp.concatenate([r[...].astype(BF16) for r in lat_pages[c0:c0 + chunk]], axis=0)
        key_sumsq(lat_c, c0 * lat_c.shape[0] // chunk)
        chunks.append(lat_c)
    kr_t = jnp.concatenate([r[...].astype(BF16) for r in kr_pages], axis=1)
    attend(jnp.concatenate(chunks, axis=0), jnp.dot(q_rope, kr_t, preferred_element_type=F32), None)

    @pl.when(i == pl.num_programs(1) - 1)
    def _():
        first_l = lax.broadcasted_iota(jnp.int32, (LANES, ln_ref.shape[1]), 0) == 0
        first_r = lax.broadcasted_iota(jnp.int32, (LANES, rope), 0) == 0
        kr_new = jnp.where(first_r, krn_ref[...], 0.0).astype(BF16)
        lat_new = jnp.where(first_l, ln_ref[...], 0.0).astype(BF16)
        key_sumsq(lat_new, 0)
        attend(lat_new, _qk(q_rope, kr_new), 1)
        o_lat = (acc_ref[0] / _wide(l_ref[0], acc_ref.shape[-1])).astype(BF16)
        full = jnp.dot(o_lat, wuv_ref[...], preferred_element_type=F32)
        vd = full.shape[1] // heads
        own = (lax.broadcasted_iota(jnp.int32, full.shape, 1) // vd
               == lax.broadcasted_iota(jnp.int32, full.shape, 0))
        o_ref[...] = jnp.sum(jnp.where(own, full, 0.0), axis=0, keepdims=True).astype(BF16)


def _mla_decode(qcat, lat_cache, kr_cache, layer, page_table, lat_new, kr_new, w_ukt, kn_gain, w_uv,
                heads, nope, rope):
    db = qcat.shape[0]
    n_pages = page_table.shape[1]
    page, c = lat_cache.shape[2], lat_cache.shape[3]
    pages = math.gcd(n_pages, DECODE_PAGES_PER_STEP)
    ovd = w_uv.shape[1]

    def lat_spec(j):
        return pl.BlockSpec((None, None, page, c), lambda b, i, pt: (layer, pt[b * n_pages + i * pages + j], 0, 0))

    kr_cache = jnp.swapaxes(kr_cache, 2, 3)

    def kr_spec(j):
        return pl.BlockSpec((None, None, rope, page), lambda b, i, pt: (layer, pt[b * n_pages + i * pages + j], 0, 0))

    row3 = lambda b, i, pt: (b, 0, 0)
    fixed = lambda b, i, pt: (0, 0)
    grid_spec = pltpu.PrefetchScalarGridSpec(
        num_scalar_prefetch=1,
        grid=(db, n_pages // pages),
        in_specs=(
            [pl.BlockSpec((None, heads, 2 * LANES), row3)]
            + [lat_spec(j) for j in range(pages)] + [kr_spec(j) for j in range(pages)]
            + [pl.BlockSpec((None, 1, c), row3), pl.BlockSpec((None, 1, rope), row3),
               pl.BlockSpec((heads * nope, c), fixed), pl.BlockSpec((1, nope), fixed),
               pl.BlockSpec((c, ovd), fixed)]
        ),
        out_specs=pl.BlockSpec((None, 1, ovd), row3),
        scratch_shapes=[
            pltpu.VMEM((heads, c), BF16),
            pltpu.VMEM((1, heads, LANES), F32),
            pltpu.VMEM((1, heads, LANES), F32),
            pltpu.VMEM((1, heads, c), F32),
            pltpu.VMEM((heads, pages * page), F32),
        ],
    )
    return pl.pallas_call(
        functools.partial(_mla_decode_kernel, pages=pages, chunk=math.gcd(pages, MLA_DECODE_CHUNK_PAGES),
                          heads=heads, nope=nope, rope=rope),
        grid_spec=grid_spec,
        out_shape=jax.ShapeDtypeStruct((db, 1, ovd), BF16),
        compiler_params=_params("parallel", "arbitrary"),
        name="mla_decode",
    )(page_table.reshape(-1), qcat, *([lat_cache] * pages), *([kr_cache] * pages),
      lat_new.reshape(db, 1, c), kr_new.reshape(db, 1, rope), w_ukt, kn_gain.reshape(1, nope), w_uv)


def _rope_tables(pos, group):
    half = group // 2
    inv_freq = jnp.power(ROPE_THETA, -jnp.arange(half, dtype=F32) / half)
    ang = pos.astype(F32)[:, None] * inv_freq[None, :]
    cos, sin = jnp.cos(ang), jnp.sin(ang)
    rep = LANES // group
    cosf = jnp.tile(jnp.concatenate([cos, cos], axis=1), (1, rep))
    sinf = jnp.tile(jnp.concatenate([-sin, sin], axis=1), (1, rep))
    return cosf, sinf


def kernel(x_prompt, x_sample, cache_diff_k, cache_diff_v, cache_swa_k, cache_swa_v, cache_mla_latent, cache_mla_krope, page_table, norm_ffn_pre, ffn_pre_gate, ffn_pre_up, ffn_pre_down, norm_mix, norm_ffn_post, ffn_post_gate, ffn_post_up, ffn_post_down, diff_w_qkv, diff_q_gain, diff_k_gain, diff_lambda_q1, diff_lambda_k1, diff_lambda_q2, diff_lambda_k2, diff_sub_gain, diff_w_o, swa_w_qkv, swa_q_gain, swa_k_gain, swa_sinks, swa_w_o, mla_w_down, mla_q_a_gain, mla_kv_a_gain, mla_w_uq, mla_qn_gain, mla_qr_gain, mla_kn_gain, mla_kr_gain, mla_w_uk, mla_w_uv, mla_w_o):
    b, s, d = x_prompt.shape
    db, ds = x_sample.shape[0], x_sample.shape[1]
    assert ds == 1, "one new token per sampled request"
    bs = b * s
    depth = norm_ffn_pre.shape[0]
    past_len = page_table.shape[1] * cache_diff_k.shape[2]

    diff_kv = cache_diff_k.shape[3]
    diff_heads = diff_w_o.shape[1] // (2 * LANES)
    diff_group = diff_heads // diff_kv
    diff_nq, diff_nk = diff_heads * 2 * LANES, diff_kv * 2 * LANES
    assert cache_diff_k.shape[4] == 2 * LANES

    swa_kv, swa_hd = cache_swa_k.shape[3], cache_swa_k.shape[4]
    swa_heads = swa_sinks.shape[1]
    swa_group = swa_heads // swa_kv
    swa_nq, swa_nk = swa_heads * swa_hd, swa_kv * swa_hd
    assert cache_swa_k.shape[2] == WINDOW and s % WINDOW == 0 and LANES % swa_hd == 0

    mla_q_lora, mla_kv_lora = mla_w_uq.shape[1], mla_w_uk.shape[1]
    mla_heads, mla_nope = mla_w_uk.shape[2], mla_w_uk.shape[3]
    mla_rope, mla_vd = cache_mla_krope.shape[3], mla_w_uv.shape[3]
    assert mla_nope == LANES and LANES % mla_rope == 0 and mla_rope < LANES

    x = jnp.concatenate([x_prompt.reshape(bs, d), x_sample.reshape(db, d)], axis=0)
    pos = jnp.concatenate([jnp.tile(jnp.arange(s, dtype=jnp.int32), b),
                           jnp.full((db,), past_len, jnp.int32)])
    cos128, sin128 = _rope_tables(pos, LANES)
    cos64, sin64 = _rope_tables(pos, swa_hd)
    if mla_rope == swa_hd:
        cos_mr, sin_mr = cos64, sin64
    else:
        cos_mr, sin_mr = _rope_tables(pos, mla_rope)


    outs = {name: [] for name in ("dk_p", "dv_p", "dk_s", "dv_s", "wk_p", "wv_p", "wk_s", "wv_s",
                                  "ml_p", "mr_p", "ml_s", "mr_s")}
    for i in range(depth):
        kind, j = i % 3, i // 3
        x = _ffn(x, norm_ffn_pre[i], ffn_pre_gate, ffn_pre_up, ffn_pre_down, i)
        if kind == 0:
            lam_init = 0.8 - 0.6 * math.exp(-0.3 * i)
            qkv = _norm_mm(x, norm_mix[i], diff_w_qkv[j].astype(BF16), d)
            q, k, v = _diff_prep(qkv, cos128, sin128, diff_q_gain[j], diff_k_gain[j], diff_nq, diff_nk,
                                 LANES ** -0.5)
            lam_prm = (diff_lambda_q1[j], diff_lambda_k1[j], diff_lambda_q2[j], diff_lambda_k2[j])
            o_p = _diff_attn_prompt(q, k, v, lam_prm, diff_sub_gain[j], b, s, diff_kv, diff_group, lam_init)
            qs = q[bs:].reshape(db, diff_kv, diff_group, 2, 1, LANES)
            sel = jnp.eye(2, dtype=BF16).reshape(1, 1, 1, 2, 2, 1)
            qm = (qs * sel).reshape(db, diff_kv, 2 * diff_group, 2 * LANES)
            o_s = _diff_decode(qm, cache_diff_k, cache_diff_v, j, page_table, k[bs:], v[bs:], lam_prm,
                               diff_sub_gain[j], diff_group, lam_init)
            o = jnp.concatenate([o_p, o_s.reshape(db, diff_nq)], axis=0)
            x = _mm_res(o, diff_w_o[j].astype(BF16), x)
            outs["dk_p"].append(k[:bs].reshape(b, s, diff_kv, 2 * LANES))
            outs["dv_p"].append(v[:bs].reshape(b, s, diff_kv, 2 * LANES))
            outs["dk_s"].append(k[bs:].reshape(db, 1, diff_kv, 2 * LANES))
            outs["dv_s"].append(v[bs:].reshape(db, 1, diff_kv, 2 * LANES))
        elif kind == 1:
            qkv = _norm_mm(x, norm_mix[i], swa_w_qkv[j].astype(BF16), d)
            q, k, v = _swa_prep(qkv, cos64, sin64, swa_q_gain[j], swa_k_gain[j], swa_nq, swa_nk, swa_hd,
                                swa_hd ** -0.5)
            o_p = _swa_attn_prompt(q, k, v, swa_sinks[j], b, s, swa_kv, swa_group, swa_hd)
            qs = q[bs:].reshape(db, swa_kv, swa_group, 1, swa_hd)
            sel = jnp.eye(swa_kv, dtype=BF16).reshape(1, swa_kv, 1, swa_kv, 1)
            qbd = (qs * sel).reshape(db, swa_heads, swa_nk)
            o_s = _swa_decode(qbd, cache_swa_k, cache_swa_v, j, k[bs:], v[bs:], swa_sinks[j],
                              swa_kv, swa_group, swa_hd)
            o = jnp.concatenate([o_p, o_s.reshape(db, swa_nq)], axis=0)
            x = _mm_res(o, swa_w_o[j].astype(BF16), x)
            w_keep = min(WINDOW, s)
            k_p = k[:bs].reshape(b, s, swa_kv, swa_hd)
            v_p = v[:bs].reshape(b, s, swa_kv, swa_hd)
            outs["wk_p"].append(k_p[:, s - w_keep:])
            outs["wv_p"].append(v_p[:, s - w_keep:])
            outs["wk_s"].append(jnp.concatenate([cache_swa_k[j][:, 1:], k[bs:].reshape(db, 1, swa_kv, swa_hd)], axis=1))
            outs["wv_s"].append(jnp.concatenate([cache_swa_v[j][:, 1:], v[bs:].reshape(db, 1, swa_kv, swa_hd)], axis=1))
        else:
            down_n = mla_q_lora + mla_kv_lora + mla_rope
            w_down = jnp.pad(mla_w_down[j], ((0, 0), (0, LANES - mla_rope))).astype(BF16)
            dproj = _norm_mm(x, norm_mix[i], w_down, d)
            assert dproj.shape[1] == down_n + LANES - mla_rope
            lat, latb, kr_pad = _mla_down_prep(dproj, cos_mr, sin_mr, mla_kv_a_gain[j], mla_kr_gain[j],
                                               mla_q_lora, mla_kv_lora, mla_rope)
            w_uq = mla_w_uq[j].reshape(mla_q_lora, mla_heads, mla_nope + mla_rope)
            w_uq = jnp.pad(w_uq, ((0, 0), (0, 0), (0, 2 * LANES - mla_nope - mla_rope)))
            w_uq = w_uq.reshape(mla_q_lora, mla_heads * 2 * LANES).astype(BF16)
            qraw = _norm_mm(dproj, mla_q_a_gain[j], w_uq, mla_q_lora)
            qcat = _mla_q_prep(qraw, cos_mr, sin_mr, mla_qn_gain[j], mla_qr_gain[j], mla_heads, mla_rope,
                               (mla_nope + mla_rope) ** -0.5)
            w_uk = mla_w_uk[j].reshape(mla_kv_lora, mla_heads * mla_nope).astype(BF16)
            w_uv = mla_w_uv[j].reshape(mla_kv_lora, mla_heads * mla_vd).astype(BF16)
            kcat = _mla_k_prep(latb, kr_pad, w_uk, mla_kn_gain[j], mla_heads)
            o_p = _mla_attn_prompt(qcat, kcat, latb, w_uv, b, s, mla_heads)
            kr = kr_pad[:, :mla_rope]
            o_s = _mla_decode(qcat[bs:].reshape(db, mla_heads, 2 * LANES), cache_mla_latent, cache_mla_krope, j,
                              page_table, lat[bs:], kr[bs:], w_uk.T, mla_kn_gain[j], w_uv,
                              mla_heads, mla_nope, mla_rope)
            o = jnp.concatenate([o_p, o_s.reshape(db, mla_heads * mla_vd)], axis=0)
            x = _mm_res(o, mla_w_o[j].astype(BF16), x)
            outs["ml_p"].append(lat[:bs].reshape(b, s, mla_kv_lora))
            outs["mr_p"].append(kr[:bs].reshape(b, s, mla_rope))
            outs["ml_s"].append(lat[bs:].reshape(db, 1, mla_kv_lora))
            outs["mr_s"].append(kr[bs:].reshape(db, 1, mla_rope))
        x = _ffn(x, norm_ffn_post[i], ffn_post_gate, ffn_post_up, ffn_post_down, i)

    return (x[:bs].reshape(b, s, d), x[bs:].reshape(db, 1, d),
            *[jnp.stack(outs[name]) for name in ("dk_p", "dv_p", "dk_s", "dv_s", "wk_p", "wv_p", "wk_s", "wv_s",
                                                 "ml_p", "mr_p", "ml_s", "mr_s")])
```

```python
import functools
import math

import jax
import jax.numpy as jnp
from jax import lax
from jax.experimental import pallas as pl
from jax.experimental.pallas import tpu as pltpu

NORM_EPS = 1e-6
ROPE_THETA = 10000.0
WINDOW = 128
LANES = 128
VMEM_LIMIT_BYTES = 56 * 1024 * 1024
DECODE_PAGES_PER_STEP = 16
MLA_DECODE_CHUNK_PAGES = 4
FFN_MAX_TOKEN_TILE = 1280
BF16 = jnp.bfloat16
F32 = jnp.float32
NEG_INF = float("-inf")


def _pick(n, cap, mults=(128, 16, 8)):
    for mult in mults:
        best = 0
        for d in range(mult, min(n, cap) + 1, mult):
            if n % d == 0:
                best = d
        if best:
            return best
    return n


def _params(*sem):
    return pltpu.CompilerParams(dimension_semantics=sem, vmem_limit_bytes=VMEM_LIMIT_BYTES)


def _rms(x, gain):
    ms = jnp.mean(x * x, axis=-1, keepdims=True)
    return x * lax.rsqrt(ms + NORM_EPS) * gain


def _ffn_kernel(x_ref, g_ref, wg_ref, wu_ref, wd_ref, o_ref, h_ref):
    @pl.when(pl.program_id(1) == 0)
    def _():
        x = x_ref[...]
        h_ref[...] = _rms(x, g_ref[...]).astype(BF16)
        o_ref[...] = x

    h = h_ref[...]
    g = jnp.dot(h, wg_ref[...].astype(BF16), preferred_element_type=F32)
    u = jnp.dot(h, wu_ref[...].astype(BF16), preferred_element_type=F32)
    act = g * (0.5 / (1.0 + jnp.exp(-g))) * u
    o_ref[...] += jnp.dot(act.astype(BF16), wd_ref[...].astype(BF16), preferred_element_type=F32)


def _ffn(x, gain, wg, wu, wd, layer):
    t, d = x.shape
    f = wg.shape[2]
    tm = _pick(t, FFN_MAX_TOKEN_TILE, (16,))
    tf = _pick(f, 256)
    once = pl.Buffered(1)
    return pl.pallas_call(
        _ffn_kernel,
        grid=(t // tm, f // tf),
        in_specs=[
            pl.BlockSpec((tm, d), lambda i, j: (i, 0), pipeline_mode=once),
            pl.BlockSpec((1, d), lambda i, j: (0, 0)),
            pl.BlockSpec((None, d, tf), lambda i, j: (layer, 0, j)),
            pl.BlockSpec((None, d, tf), lambda i, j: (layer, 0, j)),
            pl.BlockSpec((None, tf, d), lambda i, j: (layer, j, 0)),
        ],
        out_specs=pl.BlockSpec((tm, d), lambda i, j: (i, 0), pipeline_mode=once),
        out_shape=jax.ShapeDtypeStruct((t, d), F32),
        scratch_shapes=[pltpu.VMEM((tm, d), BF16)],
        compiler_params=_params("parallel", "arbitrary"),
        name="ffn",
    )(x, gain.reshape(1, d), wg, wu, wd)


def _norm_mm_kernel(x_ref, g_ref, w_ref, o_ref, h_ref):
    @pl.when(pl.program_id(1) == 0)
    def _():
        h_ref[...] = _rms(x_ref[...], g_ref[...]).astype(BF16)

    o_ref[...] = jnp.dot(h_ref[...], w_ref[...], preferred_element_type=F32)


def _norm_mm(x, gain, w, k):
    t = x.shape[0]
    n = w.shape[1]
    tm = _pick(t, 768)
    tn = _pick(n, 1024)
    return pl.pallas_call(
        _norm_mm_kernel,
        grid=(t // tm, n // tn),
        in_specs=[
            pl.BlockSpec((tm, k), lambda i, j: (i, 0)),
            pl.BlockSpec((1, k), lambda i, j: (0, 0)),
            pl.BlockSpec((k, tn), lambda i, j: (0, j)),
        ],
        out_specs=pl.BlockSpec((tm, tn), lambda i, j: (i, j)),
        out_shape=jax.ShapeDtypeStruct((t, n), F32),
        scratch_shapes=[pltpu.VMEM((tm, k), BF16)],
        compiler_params=_params("parallel", "arbitrary"),
        name="norm_mm",
    )(x, gain.reshape(1, k), w)


def _mm_res_kernel(a_ref, w_ref, x_ref, o_ref):
    o_ref[...] = x_ref[...] + jnp.dot(a_ref[...], w_ref[...], preferred_element_type=F32)


def _mm_res(a, w, x):
    t, k = a.shape
    n = w.shape[1]
    tm = _pick(t, 768)
    tn = _pick(n, 1024)
    return pl.pallas_call(
        _mm_res_kernel,
        grid=(t // tm, n // tn),
        in_specs=[
            pl.BlockSpec((tm, k), lambda i, j: (i, 0)),
            pl.BlockSpec((k, tn), lambda i, j: (0, j)),
            pl.BlockSpec((tm, tn), lambda i, j: (i, j)),
        ],
        out_specs=pl.BlockSpec((tm, tn), lambda i, j: (i, j)),
        out_shape=jax.ShapeDtypeStruct((t, n), F32),
        compiler_params=_params("parallel", "parallel"),
        name="mm_res",
    )(a, w, x)


def _head_norm(x, gain, group):
    xx = x * x
    if group == LANES:
        ms = jnp.mean(xx, axis=-1, keepdims=True)
    else:
        lo = lax.broadcasted_iota(jnp.int32, x.shape, 1) < group
        s_lo = jnp.sum(jnp.where(lo, xx, 0.0), axis=-1, keepdims=True)
        s_hi = jnp.sum(jnp.where(lo, 0.0, xx), axis=-1, keepdims=True)
        ms = jnp.where(lo, s_lo, s_hi) * (1.0 / group)
    return x * lax.rsqrt(ms + NORM_EPS) * gain


def _rope(y, cosf, sinf, group):
    if group == LANES:
        rot = pltpu.roll(y, LANES // 2, 1)
    else:
        lane = lax.broadcasted_iota(jnp.int32, y.shape, 1)
        first_half = (lane % group) < (group // 2)
        rot = jnp.where(first_half, pltpu.roll(y, LANES - group // 2, 1), pltpu.roll(y, group // 2, 1))
    return y * cosf + rot * sinf


def _diff_prep_kernel(qkv_ref, cos_ref, sin_ref, qg_ref, kg_ref, q_ref, k_ref, v_ref, *, nq, nk, scale):
    cosf, sinf = cos_ref[...], sin_ref[...]
    for c in range(nq):
        y = _head_norm(qkv_ref[:, c * LANES:(c + 1) * LANES], qg_ref[...], LANES)
        q_ref[:, c * LANES:(c + 1) * LANES] = (_rope(y, cosf, sinf, LANES) * scale).astype(BF16)
    for c in range(nk):
        y = _head_norm(qkv_ref[:, (nq + c) * LANES:(nq + c + 1) * LANES], kg_ref[...], LANES)
        k_ref[:, c * LANES:(c + 1) * LANES] = _rope(y, cosf, sinf, LANES)
    v_ref[...] = qkv_ref[:, (nq + nk) * LANES:]


def _diff_prep(qkv, cosf, sinf, q_gain, k_gain, nq_cols, nk_cols, scale):
    t, n = qkv.shape
    tm = _pick(t, 768)
    nv_cols = n - nq_cols - nk_cols
    row = lambda i: (i, 0)
    fixed = lambda i: (0, 0)
    return pl.pallas_call(
        functools.partial(_diff_prep_kernel, nq=nq_cols // LANES, nk=nk_cols // LANES, scale=scale),
        grid=(t // tm,),
        in_specs=[
            pl.BlockSpec((tm, n), row),
            pl.BlockSpec((tm, LANES), row),
            pl.BlockSpec((tm, LANES), row),
            pl.BlockSpec((1, LANES), fixed),
            pl.BlockSpec((1, LANES), fixed),
        ],
        out_specs=[
            pl.BlockSpec((tm, nq_cols), row),
            pl.BlockSpec((tm, nk_cols), row),
            pl.BlockSpec((tm, nv_cols), row),
        ],
        out_shape=[
            jax.ShapeDtypeStruct((t, nq_cols), BF16),
            jax.ShapeDtypeStruct((t, nk_cols), F32),
            jax.ShapeDtypeStruct((t, nv_cols), F32),
        ],
        compiler_params=_params("parallel"),
        name="diff_prep",
    )(qkv, cosf, sinf, q_gain.reshape(1, LANES), k_gain.reshape(1, LANES))


def _swa_prep_kernel(qkv_ref, cos_ref, sin_ref, qg_ref, kg_ref, q_ref, k_ref, v_ref, *, nq, nk, group, scale):
    cosf, sinf = cos_ref[...], sin_ref[...]
    for c in range(nq):
        y = _head_norm(qkv_ref[:, c * LANES:(c + 1) * LANES], qg_ref[...], group)
        q_ref[:, c * LANES:(c + 1) * LANES] = (_rope(y, cosf, sinf, group) * scale).astype(BF16)
    for c in range(nk):
        y = _head_norm(qkv_ref[:, (nq + c) * LANES:(nq + c + 1) * LANES], kg_ref[...], group)
        k_ref[:, c * LANES:(c + 1) * LANES] = _rope(y, cosf, sinf, group)
    v_ref[...] = qkv_ref[:, (nq + nk) * LANES:]


def _swa_prep(qkv, cosf, sinf, q_gain, k_gain, nq_cols, nk_cols, group, scale):
    t, n = qkv.shape
    tm = _pick(t, 768)
    nv_cols = n - nq_cols - nk_cols
    rep = LANES // group
    row = lambda i: (i, 0)
    fixed = lambda i: (0, 0)
    return pl.pallas_call(
        functools.partial(_swa_prep_kernel, nq=nq_cols // LANES, nk=nk_cols // LANES, group=group, scale=scale),
        grid=(t // tm,),
        in_specs=[
            pl.BlockSpec((tm, n), row),
            pl.BlockSpec((tm, LANES), row),
            pl.BlockSpec((tm, LANES), row),
            pl.BlockSpec((1, LANES), fixed),
            pl.BlockSpec((1, LANES), fixed),
        ],
        out_specs=[
            pl.BlockSpec((tm, nq_cols), row),
            pl.BlockSpec((tm, nk_cols), row),
            pl.BlockSpec((tm, nv_cols), row),
        ],
        out_shape=[
            jax.ShapeDtypeStruct((t, nq_cols), BF16),
            jax.ShapeDtypeStruct((t, nk_cols), F32),
            jax.ShapeDtypeStruct((t, nv_cols), F32),
        ],
        compiler_params=_params("parallel"),
        name="swa_prep",
    )(qkv, cosf, sinf, jnp.tile(q_gain, rep).reshape(1, LANES), jnp.tile(k_gain, rep).reshape(1, LANES))


def _mla_down_prep_kernel(d_ref, cos_ref, sin_ref, kvg_ref, krg_ref, lat_ref, latb_ref, kr_ref, *, q_lora, kv_lora, rope):
    lat = _rms(d_ref[:, q_lora:q_lora + kv_lora], kvg_ref[...])
    lat_ref[...] = lat
    latb_ref[...] = lat.astype(BF16)
    y = _head_norm(d_ref[:, q_lora + kv_lora:q_lora + kv_lora + LANES], krg_ref[...], rope)
    kr_ref[...] = _rope(y, cos_ref[...], sin_ref[...], rope)


def _mla_down_prep(d, cosf, sinf, kv_gain, kr_gain, q_lora, kv_lora, rope):
    t, n = d.shape
    tm = _pick(t, 768)
    row = lambda i: (i, 0)
    fixed = lambda i: (0, 0)
    krg = jnp.concatenate([kr_gain, jnp.zeros((LANES - rope,), F32)]).reshape(1, LANES)
    return pl.pallas_call(
        functools.partial(_mla_down_prep_kernel, q_lora=q_lora, kv_lora=kv_lora, rope=rope),
        grid=(t // tm,),
        in_specs=[
            pl.BlockSpec((tm, n), row),
            pl.BlockSpec((tm, LANES), row),
            pl.BlockSpec((tm, LANES), row),
            pl.BlockSpec((1, kv_lora), fixed),
            pl.BlockSpec((1, LANES), fixed),
        ],
        out_specs=[
            pl.BlockSpec((tm, kv_lora), row),
            pl.BlockSpec((tm, kv_lora), row),
            pl.BlockSpec((tm, LANES), row),
        ],
        out_shape=[
            jax.ShapeDtypeStruct((t, kv_lora), F32),
            jax.ShapeDtypeStruct((t, kv_lora), BF16),
            jax.ShapeDtypeStruct((t, LANES), F32),
        ],
        compiler_params=_params("parallel"),
        name="mla_down_prep",
    )(d, cosf, sinf, kv_gain.reshape(1, kv_lora), krg)


def _mla_q_prep_kernel(q_ref, cos_ref, sin_ref, ng_ref, rg_ref, o_ref, *, heads, rope, scale):
    cosf, sinf = cos_ref[...], sin_ref[...]
    for h in range(heads):
        lo = 2 * h * LANES
        y = _head_norm(q_ref[:, lo:lo + LANES], ng_ref[...], LANES)
        o_ref[:, lo:lo + LANES] = (y * scale).astype(BF16)
        y = _head_norm(q_ref[:, lo + LANES:lo + 2 * LANES], rg_ref[...], rope)
        o_ref[:, lo + LANES:lo + 2 * LANES] = (_rope(y, cosf, sinf, rope) * scale).astype(BF16)


def _mla_q_prep(q, cosf, sinf, qn_gain, qr_gain, heads, rope, scale):
    t, n = q.shape
    tm = _pick(t, 768)
    row = lambda i: (i, 0)
    fixed = lambda i: (0, 0)
    rg = jnp.concatenate([qr_gain, jnp.zeros((LANES - rope,), F32)]).reshape(1, LANES)
    return pl.pallas_call(
        functools.partial(_mla_q_prep_kernel, heads=heads, rope=rope, scale=scale),
        grid=(t // tm,),
        in_specs=[
            pl.BlockSpec((tm, n), row),
            pl.BlockSpec((tm, LANES), row),
            pl.BlockSpec((tm, LANES), row),
            pl.BlockSpec((1, LANES), fixed),
            pl.BlockSpec((1, LANES), fixed),
        ],
        out_specs=pl.BlockSpec((tm, n), row),
        out_shape=jax.ShapeDtypeStruct((t, n), BF16),
        compiler_params=_params("parallel"),
        name="mla_q_prep",
    )(q, cosf, sinf, qn_gain.reshape(1, LANES), rg)


def _mla_k_prep_kernel(lat_ref, kr_ref, w_ref, g_ref, o_ref, *, heads):
    y = jnp.dot(lat_ref[...], w_ref[...], preferred_element_type=F32)
    kr = kr_ref[...].astype(BF16)
    for h in range(heads):
        o_ref[:, 2 * h * LANES:(2 * h + 1) * LANES] = _rms(y[:, h * LANES:(h + 1) * LANES], g_ref[...]).astype(BF16)
        o_ref[:, (2 * h + 1) * LANES:(2 * h + 2) * LANES] = kr


def _mla_k_prep(latb, kr_pad, w_uk, kn_gain, heads):
    t, c = latb.shape
    tm = _pick(t, 768)
    row = lambda i: (i, 0)
    fixed = lambda i: (0, 0)
    return pl.pallas_call(
        functools.partial(_mla_k_prep_kernel, heads=heads),
        grid=(t // tm,),
        in_specs=[
            pl.BlockSpec((tm, c), row),
            pl.BlockSpec((tm, LANES), row),
            pl.BlockSpec((c, heads * LANES), fixed),
            pl.BlockSpec((1, LANES), fixed),
        ],
        out_specs=pl.BlockSpec((tm, 2 * heads * LANES), row),
        out_shape=jax.ShapeDtypeStruct((t, 2 * heads * LANES), BF16),
        compiler_params=_params("parallel"),
        name="mla_k_prep",
    )(latb, kr_pad, w_uk, kn_gain.reshape(1, LANES))


def _lanes(col, width=LANES):
    return jnp.broadcast_to(col, (col.shape[0], width))


def _wide(rep, width):
    return rep if width == LANES else jnp.tile(rep, (1, width // LANES))


def _online_update(s, pv, m_ref, l_ref, acc_ref, idx):
    keys = s.shape[1]
    m_prev = m_ref[idx]
    m_new = jnp.maximum(m_prev, _lanes(jnp.max(s, axis=-1, keepdims=True)))
    alpha = jnp.exp(m_prev - m_new)
    p = jnp.exp(s - _wide(m_new, keys))
    l_ref[idx] = alpha * l_ref[idx] + _lanes(jnp.sum(p, axis=-1, keepdims=True))
    acc_ref[idx] = _wide(alpha, acc_ref.shape[-1]) * acc_ref[idx] + pv(p)
    m_ref[idx] = m_new


def _pv(v):
    return lambda p: jnp.dot(p.astype(BF16), v, preferred_element_type=F32)


def _qk(q, k):
    return lax.dot_general(q, k, (((1,), (1,)), ((), ())), preferred_element_type=F32)


def _diff_lambda(lq1, lk1, lq2, lk2, lam_init):
    return (jnp.exp(jnp.sum(lq1[...] * lk1[...], axis=-1, keepdims=True))
            - jnp.exp(jnp.sum(lq2[...] * lk2[...], axis=-1, keepdims=True)) + lam_init)


def _diff_attn_kernel(q_ref, k_ref, v_ref, lq1, lk1, lq2, lk2, sg_ref, o_ref, m_ref, l_ref, acc_ref,
                      *, tq, group, lam_init):
    qi = pl.program_id(2)
    m_ref[...] = jnp.full(m_ref.shape, NEG_INF, F32)
    l_ref[...] = jnp.zeros(l_ref.shape, F32)
    acc_ref[...] = jnp.zeros(acc_ref.shape, F32)
    hd = LANES
    causal = (lax.broadcasted_iota(jnp.int32, (tq, tq), 0) >= lax.broadcasted_iota(jnp.int32, (tq, tq), 1))

    def step(ki, masked):
        ks = pl.multiple_of(ki * tq, tq)
        kb = k_ref[pl.ds(ks, tq), :].astype(BF16)
        vb = v_ref[pl.ds(ks, tq), :].astype(BF16)
        for idx in range(2 * group):
            c = idx % 2
            s = _qk(q_ref[:, idx * hd:(idx + 1) * hd], kb[:, c * hd:(c + 1) * hd])
            if masked:
                s = jnp.where(causal, s, NEG_INF)
            _online_update(s, _pv(vb), m_ref, l_ref, acc_ref, idx)

    def body(ki, carry):
        step(ki, False)
        return carry

    lax.fori_loop(0, qi, body, 0)
    step(qi, True)

    lam = _diff_lambda(lq1, lk1, lq2, lk2, lam_init)
    for g in range(group):
        o = (acc_ref[2 * g] / _wide(l_ref[2 * g], 2 * hd)
             - lam * (acc_ref[2 * g + 1] / _wide(l_ref[2 * g + 1], 2 * hd)))
        o = _rms(o, sg_ref[...]) * (1.0 - lam_init)
        o_ref[:, g * 2 * hd:(g + 1) * 2 * hd] = o.astype(BF16)


def _diff_attn_prompt(q, k, v, lam_prm, sub_gain, b, s, kv_heads, group, lam_init):
    hd = LANES
    tq = _pick(s, 512, (128,))
    nq = s // tq
    qw = group * 2 * hd
    vec = lambda *_: (0, 0)
    return pl.pallas_call(
        functools.partial(_diff_attn_kernel, tq=tq, group=group, lam_init=lam_init),
        grid=(b, kv_heads, nq),
        in_specs=[
            pl.BlockSpec((tq, qw), lambda bi, h, qi: (bi * nq + qi, h)),
            pl.BlockSpec((s, 2 * hd), lambda bi, h, qi: (bi, h)),
            pl.BlockSpec((s, 2 * hd), lambda bi, h, qi: (bi, h)),
            pl.BlockSpec((1, hd), vec), pl.BlockSpec((1, hd), vec),
            pl.BlockSpec((1, hd), vec), pl.BlockSpec((1, hd), vec),
            pl.BlockSpec((1, 2 * hd), vec),
        ],
        out_specs=pl.BlockSpec((tq, qw), lambda bi, h, qi: (bi * nq + qi, h)),
        out_shape=jax.ShapeDtypeStruct((b * s, kv_heads * qw), BF16),
        scratch_shapes=[
            pltpu.VMEM((2 * group, tq, LANES), F32),
            pltpu.VMEM((2 * group, tq, LANES), F32),
            pltpu.VMEM((2 * group, tq, 2 * hd), F32),
        ],
        compiler_params=_params("parallel", "parallel", "arbitrary"),
        name="diff_attn_prompt",
    )(q, k, v, *[p.reshape(1, hd) for p in lam_prm], sub_gain.reshape(1, 2 * hd))


def _diff_decode_kernel(pt_ref, qm_ref, *refs, pages, page, kv_heads, group, lam_init):
    k_pages = refs[:pages]
    v_pages = refs[pages:2 * pages]
    kn_ref, vn_ref, lq1, lk1, lq2, lk2, sg_ref, o_ref, m_ref, l_ref, acc_ref = refs[2 * pages:]
    i = pl.program_id(1)
    hw = 2 * LANES

    def head_rows(page_refs, h):
        halves = [jnp.concatenate([r[pl.ds(c * kv_heads + h, page, stride=2 * kv_heads), :] for r in page_refs], axis=0)
                  for c in range(2)]
        return jnp.concatenate(halves, axis=1).astype(BF16)

    @pl.when(i == 0)
    def _():
        m_ref[...] = jnp.full(m_ref.shape, NEG_INF, F32)
        l_ref[...] = jnp.zeros(l_ref.shape, F32)
        acc_ref[...] = jnp.zeros(acc_ref.shape, F32)

    gr = 2 * group

    def scores(keys):
        return jnp.concatenate([_qk(qm_ref[h], keys[h]) for h in range(kv_heads)], axis=0)

    def values(vals):
        return lambda p: jnp.concatenate(
            [jnp.dot(p[h * gr:(h + 1) * gr].astype(BF16), vals[h], preferred_element_type=F32)
             for h in range(kv_heads)], axis=0)

    _online_update(scores([head_rows(k_pages, h) for h in range(kv_heads)]),
                   values([head_rows(v_pages, h) for h in range(kv_heads)]), m_ref, l_ref, acc_ref, 0)

    @pl.when(i == pl.num_programs(1) - 1)
    def _():
        first = lax.broadcasted_iota(jnp.int32, (LANES, hw), 0) == 0
        s = scores([jnp.where(first, kn_ref[:, h * hw:(h + 1) * hw], 0.0).astype(BF16) for h in range(kv_heads)])
        s = jnp.where(lax.broadcasted_iota(jnp.int32, s.shape, 1) == 0, s, NEG_INF)
        _online_update(s, values([jnp.where(first, vn_ref[:, h * hw:(h + 1) * hw], 0.0).astype(BF16)
                                  for h in range(kv_heads)]), m_ref, l_ref, acc_ref, 0)
        lam = _diff_lambda(lq1, lk1, lq2, lk2, lam_init)
        a = acc_ref[0] / _wide(l_ref[0], hw)
        for hg in range(kv_heads * group):
            o = a[2 * hg:2 * hg + 1] - lam * a[2 * hg + 1:2 * hg + 2]
            o = _rms(o, sg_ref[...]) * (1.0 - lam_init)
            o_ref[:, hg * hw:(hg + 1) * hw] = o.astype(BF16)


def _diff_decode(qm, k_cache, v_cache, layer, page_table, k_new, v_new, lam_prm, sub_gain, group, lam_init):
    db, kv_heads = qm.shape[0], qm.shape[1]
    n_pages = page_table.shape[1]
    hw = 2 * LANES
    width = kv_heads * hw
    page = k_cache.shape[2]
    pages = math.gcd(n_pages, DECODE_PAGES_PER_STEP)
    layers, pool = k_cache.shape[0], k_cache.shape[1]
    rows = page * 2 * kv_heads

    def page_rows(cache):
        c = cache.reshape(layers, pool, page, kv_heads, 2, LANES)
        return c.transpose(0, 1, 2, 4, 3, 5).reshape(layers, pool, rows, LANES)

    kc, vc = page_rows(k_cache), page_rows(v_cache)

    def page_spec(j):
        return pl.BlockSpec((None, None, rows, LANES),
                            lambda b, i, pt: (layer, pt[b * n_pages + i * pages + j], 0, 0))

    row3 = lambda b, i, pt: (b, 0, 0)
    vec = lambda b, i, pt: (0, 0)
    grid_spec = pltpu.PrefetchScalarGridSpec(
        num_scalar_prefetch=1,
        grid=(db, n_pages // pages),
        in_specs=(
            [pl.BlockSpec((None, kv_heads, 2 * group, hw), lambda b, i, pt: (b, 0, 0, 0))]
            + [page_spec(j) for j in range(pages)] + [page_spec(j) for j in range(pages)]
            + [pl.BlockSpec((None, 1, width), row3), pl.BlockSpec((None, 1, width), row3)]
            + [pl.BlockSpec((1, LANES), vec)] * 4 + [pl.BlockSpec((1, hw), vec)]
        ),
        out_specs=pl.BlockSpec((None, 1, kv_heads * group * hw), row3),
        scratch_shapes=[
            pltpu.VMEM((1, kv_heads * 2 * group, LANES), F32),
            pltpu.VMEM((1, kv_heads * 2 * group, LANES), F32),
            pltpu.VMEM((1, kv_heads * 2 * group, hw), F32),
        ],
    )
    return pl.pallas_call(
        functools.partial(_diff_decode_kernel, pages=pages, page=page, kv_heads=kv_heads, group=group,
                          lam_init=lam_init),
        grid_spec=grid_spec,
        out_shape=jax.ShapeDtypeStruct((db, 1, kv_heads * group * hw), BF16),
        compiler_params=_params("parallel", "arbitrary"),
        name="diff_decode",
    )(page_table.reshape(-1), qm, *([kc] * pages), *([vc] * pages),
      k_new.reshape(db, 1, width), v_new.reshape(db, 1, width),
      *[p.reshape(1, LANES) for p in lam_prm], sub_gain.reshape(1, hw))


def _swa_attn_kernel(q_ref, kp_ref, kc_ref, vp_ref, vc_ref, sink_ref, o_ref, *, kv_heads, group, hd, win):
    blk = pl.program_id(1)
    r = lax.broadcasted_iota(jnp.int32, (win, 2 * win), 0)
    c = lax.broadcasted_iota(jnp.int32, (win, 2 * win), 1)
    dist = win + r - c
    ok = (dist >= 0) & (dist <= win) & ((c >= win) | (blk > 0))
    bias = jnp.tile(jnp.where(ok, 0.0, NEG_INF), (group, 1))
    low_q = lax.broadcasted_iota(jnp.int32, (win, LANES), 1) < hd
    keep = (jnp.where(low_q, 1.0, 0.0).astype(BF16), jnp.where(low_q, 0.0, 1.0).astype(BF16))
    low_k = lax.broadcasted_iota(jnp.int32, (2 * win, LANES), 1) < hd
    kcat = jnp.concatenate([kp_ref[...], kc_ref[...]], axis=0)
    vcat = jnp.concatenate([vp_ref[...], vc_ref[...]], axis=0)

    def both_halves(chunk, half):
        swapped = pltpu.roll(chunk, hd, 1)
        return (jnp.where(low_k, chunk, swapped) if half == 0 else jnp.where(low_k, swapped, chunk)).astype(BF16)

    for j in range(kv_heads):
        t, half = divmod(j, 2)
        kk = both_halves(kcat[:, t * LANES:(t + 1) * LANES], half)
        vv = both_halves(vcat[:, t * LANES:(t + 1) * LANES], half)
        heads = [j * group + g for g in range(group)]
        q = jnp.concatenate([q_ref[:, (h // 2) * LANES:(h // 2 + 1) * LANES] * keep[h % 2] for h in heads], axis=0)
        sink = jnp.concatenate([jnp.broadcast_to(sink_ref[h:h + 1, :], (win, LANES)) for h in heads], axis=0)
        s = _qk(q, kk) + bias
        m = jnp.maximum(_lanes(jnp.max(s, axis=-1, keepdims=True)), sink)
        p = jnp.exp(s - _wide(m, 2 * win))
        den = _lanes(jnp.sum(p, axis=-1, keepdims=True)) + jnp.exp(sink - m)
        o = jnp.dot(p.astype(BF16), vv, preferred_element_type=F32) / den
        for gp in range(group // 2):
            pair = jnp.where(low_q, o[2 * gp * win:(2 * gp + 1) * win], o[(2 * gp + 1) * win:(2 * gp + 2) * win])
            chunk = j * group // 2 + gp
            o_ref[:, chunk * LANES:(chunk + 1) * LANES] = pair.astype(BF16)


def _swa_attn_prompt(q, k, v, sinks, b, s, kv_heads, group, hd):
    win = WINDOW
    nb = s // win
    heads = kv_heads * group
    assert 2 * hd == LANES and group % 2 == 0 and kv_heads % 2 == 0, "two heads per 128-lane chunk"
    cur = lambda bi, i: (bi * nb + i, 0)
    prev = lambda bi, i: (bi * nb + jnp.maximum(i - 1, 0), 0)
    return pl.pallas_call(
        functools.partial(_swa_attn_kernel, kv_heads=kv_heads, group=group, hd=hd, win=win),
        grid=(b, nb),
        in_specs=[
            pl.BlockSpec((win, heads * hd), cur),
            pl.BlockSpec((win, kv_heads * hd), prev),
            pl.BlockSpec((win, kv_heads * hd), cur),
            pl.BlockSpec((win, kv_heads * hd), prev),
            pl.BlockSpec((win, kv_heads * hd), cur),
            pl.BlockSpec((heads, LANES), lambda bi, i: (0, 0)),
        ],
        out_specs=pl.BlockSpec((win, heads * hd), cur),
        out_shape=jax.ShapeDtypeStruct((b * s, heads * hd), BF16),
        compiler_params=_params("parallel", "parallel"),
        name="swa_attn_prompt",
    )(q, k, k, v, v, jnp.broadcast_to(sinks.reshape(heads, 1), (heads, LANES)))


def _swa_decode_kernel(q_ref, k_ref, v_ref, kn_ref, vn_ref, sink_ref, o_ref, *, bb, kv_heads, group, hd):
    heads = kv_heads * group
    width = kv_heads * hd
    own = (lax.broadcasted_iota(jnp.int32, (heads, width), 1) // hd
           == lax.broadcasted_iota(jnp.int32, (heads, width), 0) // group)
    sink = sink_ref[...]
    for bi in range(bb):
        q = q_ref[bi]
        s = jnp.dot(q, k_ref[bi].astype(BF16), preferred_element_type=F32)
        s_new = jnp.sum(q.astype(F32) * kn_ref[bi:bi + 1, :], axis=-1, keepdims=True)
        m = jnp.maximum(jnp.maximum(jnp.max(s, axis=-1, keepdims=True), s_new), sink)
        p = jnp.exp(s - m)
        p_new = jnp.exp(s_new - m)
        den = jnp.sum(p, axis=-1, keepdims=True) + p_new + jnp.exp(sink - m)
        o = _qk(p.astype(BF16), v_ref[bi].astype(BF16)) + p_new * vn_ref[bi:bi + 1, :]
        o = jnp.where(own, o / den, 0.0)
        acc = o[:, 0:hd]
        for j in range(1, kv_heads):
            acc = acc + o[:, j * hd:(j + 1) * hd]
        o_ref[bi] = acc.astype(BF16)


def _swa_decode(qbd, k_cache, v_cache, layer, k_new, v_new, sinks, kv_heads, group, hd):
    db, heads, width = qbd.shape
    win = k_cache.shape[2]
    bb = _pick(db, 16, (16, 8))
    kc = k_cache.transpose(0, 1, 3, 4, 2).reshape(k_cache.shape[0], db, width, win)
    vc = v_cache.transpose(0, 1, 3, 4, 2).reshape(kc.shape)
    return pl.pallas_call(
        functools.partial(_swa_decode_kernel, bb=bb, kv_heads=kv_heads, group=group, hd=hd),
        grid=(db // bb,),
        in_specs=[
            pl.BlockSpec((bb, heads, width), lambda i: (i, 0, 0)),
            pl.BlockSpec((None, bb, width, win), lambda i: (layer, i, 0, 0)),
            pl.BlockSpec((None, bb, width, win), lambda i: (layer, i, 0, 0)),
            pl.BlockSpec((bb, width), lambda i: (i, 0)),
            pl.BlockSpec((bb, width), lambda i: (i, 0)),
            pl.BlockSpec((heads, 1), lambda i: (0, 0)),
        ],
        out_specs=pl.BlockSpec((bb, heads, hd), lambda i: (i, 0, 0)),
        out_shape=jax.ShapeDtypeStruct((db, heads, hd), BF16),
        compiler_params=_params("parallel"),
        name="swa_decode",
    )(qbd, kc, vc, k_new, v_new, sinks.reshape(heads, 1))


def _mla_attn_kernel(q_ref, k_ref, lat_ref, wuv_ref, o_ref, m_ref, l_ref, acc_ref, *, tq, hp, vd):
    qi = pl.program_id(2)
    m_ref[...] = jnp.full(m_ref.shape, NEG_INF, F32)
    l_ref[...] = jnp.zeros(l_ref.shape, F32)
    acc_ref[...] = jnp.zeros(acc_ref.shape, F32)
    causal = (lax.broadcasted_iota(jnp.int32, (tq, tq), 0) >= lax.broadcasted_iota(jnp.int32, (tq, tq), 1))
    kw = 2 * LANES

    def step(ki, masked):
        ks = pl.multiple_of(ki * tq, tq)
        lat = lat_ref[pl.ds(ks, tq), :]
        for h in range(hp):
            s = _qk(q_ref[:, h * kw:(h + 1) * kw], k_ref[pl.ds(ks, tq), h * kw:(h + 1) * kw])
            if masked:
                s = jnp.where(causal, s, NEG_INF)
            _online_update(s, _pv(lat), m_ref, l_ref, acc_ref, h)

    def body(ki, carry):
        step(ki, False)
        return carry

    lax.fori_loop(0, qi, body, 0)
    step(qi, True)
    for h in range(hp):
        o_lat = (acc_ref[h] / _wide(l_ref[h], acc_ref.shape[-1])).astype(BF16)
        o_ref[:, h * vd:(h + 1) * vd] = jnp.dot(o_lat, wuv_ref[:, h * vd:(h + 1) * vd],
                                                 preferred_element_type=F32).astype(BF16)


def _mla_attn_prompt(qcat, kcat, latb, w_uv, b, s, heads):
    tq = _pick(s, 512, (128,))
    nq = s // tq
    c = latb.shape[1]
    vd = w_uv.shape[1] // heads
    hp = math.gcd(heads, 4)
    return pl.pallas_call(
        functools.partial(_mla_attn_kernel, tq=tq, hp=hp, vd=vd),
        grid=(b, heads // hp, nq),
        in_specs=[
            pl.BlockSpec((tq, hp * 2 * LANES), lambda bi, h, qi: (bi * nq + qi, h)),
            pl.BlockSpec((s, hp * 2 * LANES), lambda bi, h, qi: (bi, h)),
            pl.BlockSpec((s, c), lambda bi, h, qi: (bi, 0)),
            pl.BlockSpec((c, hp * vd), lambda bi, h, qi: (0, h)),
        ],
        out_specs=pl.BlockSpec((tq, hp * vd), lambda bi, h, qi: (bi * nq + qi, h)),
        out_shape=jax.ShapeDtypeStruct((b * s, heads * vd), BF16),
        scratch_shapes=[
            pltpu.VMEM((hp, tq, LANES), F32),
            pltpu.VMEM((hp, tq, LANES), F32),
            pltpu.VMEM((hp, tq, c), F32),
        ],
        compiler_params=_params("parallel", "parallel", "arbitrary"),
        name="mla_attn_prompt",
    )(qcat, kcat, latb, w_uv)


def _mla_decode_kernel(pt_ref, q_ref, *refs, pages, chunk, heads, nope, rope):
    lat_pages = refs[:pages]
    kr_pages = refs[pages:2 * pages]
    (ln_ref, krn_ref, wukt_ref, kng_ref, wuv_ref, o_ref,
     u_ref, m_ref, l_ref, acc_ref, ss_ref) = refs[2 * pages:]
    i = pl.program_id(1)

    @pl.when(i == 0)
    def _():
        m_ref[...] = jnp.full(m_ref.shape, NEG_INF, F32)
        l_ref[...] = jnp.zeros(l_ref.shape, F32)
        acc_ref[...] = jnp.zeros(acc_ref.shape, F32)
        qg = (q_ref[:, 0:nope].astype(F32) * kng_ref[...]).astype(BF16)
        wide = jnp.concatenate([qg] * heads, axis=1)
        own = (lax.broadcasted_iota(jnp.int32, wide.shape, 1) // nope
               == lax.broadcasted_iota(jnp.int32, wide.shape, 0))
        qbd = jnp.where(own, wide, jnp.zeros_like(wide))
        u_ref[...] = jnp.dot(qbd, wukt_ref[...], preferred_element_type=F32).astype(BF16)

    q_rope = q_ref[:, nope:nope + rope]

    def key_sumsq(lat, col):
        n = lat.shape[0]
        yt = _qk(wukt_ref[...], lat)
        for h in range(heads):
            yh = yt[h * nope:(h + 1) * nope]
            ss_ref[h:h + 1, col:col + n] = jnp.sum(yh * yh, axis=0, keepdims=True)

    def attend(lat, s_rope, valid):
        n = lat.shape[0]
        inv = lax.rsqrt(ss_ref[:, 0:n] * (1.0 / nope) + NORM_EPS)
        s = _qk(u_ref[...], lat) * inv + s_rope
        if valid is not None:
            s = jnp.where(lax.broadcasted_iota(jnp.int32, s.shape, 1) < valid, s, NEG_INF)
        _online_update(s, _pv(lat), m_ref, l_ref, acc_ref, 0)

    chunks = []
    for c0 in range(0, pages, chunk):
        lat_c = jnp.concatenate([r[...].astype(BF16) for r in lat_pages[c0:c0 + chunk]], axis=0)
        key_sumsq(lat_c, c0 * lat_c.shape[0] // chunk)
        chunks.append(lat_c)
    kr_t = jnp.concatenate([r[...].astype(BF16) for r in kr_pages], axis=1)
    attend(jnp.concatenate(chunks, axis=0), jnp.dot(q_rope, kr_t, preferred_element_type=F32), None)

    @pl.when(i == pl.num_programs(1) - 1)
    def _():
        first_l = lax.broadcasted_iota(jnp.int32, (LANES, ln_ref.shape[1]), 0) == 0
        first_r = lax.broadcasted_iota(jnp.int32, (LANES, rope), 0) == 0
        kr_new = jnp.where(first_r, krn_ref[...], 0.0).astype(BF16)
        lat_new = jnp.where(first_l, ln_ref[...], 0.0).astype(BF16)
        key_sumsq(lat_new, 0)
        attend(lat_new, _qk(q_rope, kr_new), 1)
        o_lat = (acc_ref[0] / _wide(l_ref[0], acc_ref.shape[-1])).astype(BF16)
        full = jnp.dot(o_lat, wuv_ref[...], preferred_element_type=F32)
        vd = full.shape[1] // heads
        own = (lax.broadcasted_iota(jnp.int32, full.shape, 1) // vd
               == lax.broadcasted_iota(jnp.int32, full.shape, 0))
        o_ref[...] = jnp.sum(jnp.where(own, full, 0.0), axis=0, keepdims=True).astype(BF16)


def _mla_decode(qcat, lat_cache, kr_cache, layer, page_table, lat_new, kr_new, w_ukt, kn_gain, w_uv,
                heads, nope, rope):
    db = qcat.shape[0]
    n_pages = page_table.shape[1]
    page, c = lat_cache.shape[2], lat_cache.shape[3]
    pages = math.gcd(n_pages, DECODE_PAGES_PER_STEP)
    ovd = w_uv.shape[1]

    def lat_spec(j):
        return pl.BlockSpec((None, None, page, c), lambda b, i, pt: (layer, pt[b * n_pages + i * pages + j], 0, 0))

    kr_cache = jnp.swapaxes(kr_cache, 2, 3)

    def kr_spec(j):
        return pl.BlockSpec((None, None, rope, page), lambda b, i, pt: (layer, pt[b * n_pages + i * pages + j], 0, 0))

    row3 = lambda b, i, pt: (b, 0, 0)
    fixed = lambda b, i, pt: (0, 0)
    grid_spec = pltpu.PrefetchScalarGridSpec(
        num_scalar_prefetch=1,
        grid=(db, n_pages // pages),
        in_specs=(
            [pl.BlockSpec((None, heads, 2 * LANES), row3)]
            + [lat_spec(j) for j in range(pages)] + [kr_spec(j) for j in range(pages)]
            + [pl.BlockSpec((None, 1, c), row3), pl.BlockSpec((None, 1, rope), row3),
               pl.BlockSpec((heads * nope, c), fixed), pl.BlockSpec((1, nope), fixed),
               pl.BlockSpec((c, ovd), fixed)]
        ),
        out_specs=pl.BlockSpec((None, 1, ovd), row3),
        scratch_shapes=[
            pltpu.VMEM((heads, c), BF16),
            pltpu.VMEM((1, heads, LANES), F32),
            pltpu.VMEM((1, heads, LANES), F32),
            pltpu.VMEM((1, heads, c), F32),
            pltpu.VMEM((heads, pages * page), F32),
        ],
    )
    return pl.pallas_call(
        functools.partial(_mla_decode_kernel, pages=pages, chunk=math.gcd(pages, MLA_DECODE_CHUNK_PAGES),
                          heads=heads, nope=nope, rope=rope),
        grid_spec=grid_spec,
        out_shape=jax.ShapeDtypeStruct((db, 1, ovd), BF16),
        compiler_params=_params("parallel", "arbitrary"),
        name="mla_decode",
    )(page_table.reshape(-1), qcat, *([lat_cache] * pages), *([kr_cache] * pages),
      lat_new.reshape(db, 1, c), kr_new.reshape(db, 1, rope), w_ukt, kn_gain.reshape(1, nope), w_uv)


def _rope_tables(pos, group):
    half = group // 2
    inv_freq = jnp.power(ROPE_THETA, -jnp.arange(half, dtype=F32) / half)
    ang = pos.astype(F32)[:, None] * inv_freq[None, :]
    cos, sin = jnp.cos(ang), jnp.sin(ang)
    rep = LANES // group
    cosf = jnp.tile(jnp.concatenate([cos, cos], axis=1), (1, rep))
    sinf = jnp.tile(jnp.concatenate([-sin, sin], axis=1), (1, rep))
    return cosf, sinf


def kernel(x_prompt, x_sample, cache_diff_k, cache_diff_v, cache_swa_k, cache_swa_v, cache_mla_latent, cache_mla_krope, page_table, norm_ffn_pre, ffn_pre_gate, ffn_pre_up, ffn_pre_down, norm_mix, norm_ffn_post, ffn_post_gate, ffn_post_up, ffn_post_down, diff_w_qkv, diff_q_gain, diff_k_gain, diff_lambda_q1, diff_lambda_k1, diff_lambda_q2, diff_lambda_k2, diff_sub_gain, diff_w_o, swa_w_qkv, swa_q_gain, swa_k_gain, swa_sinks, swa_w_o, mla_w_down, mla_q_a_gain, mla_kv_a_gain, mla_w_uq, mla_qn_gain, mla_qr_gain, mla_kn_gain, mla_kr_gain, mla_w_uk, mla_w_uv, mla_w_o):
    b, s, d = x_prompt.shape
    db, ds = x_sample.shape[0], x_sample.shape[1]
    assert ds == 1, "one new token per sampled request"
    bs = b * s
    depth = norm_ffn_pre.shape[0]
    past_len = page_table.shape[1] * cache_diff_k.shape[2]

    diff_kv = cache_diff_k.shape[3]
    diff_heads = diff_w_o.shape[1] // (2 * LANES)
    diff_group = diff_heads // diff_kv
    diff_nq, diff_nk = diff_heads * 2 * LANES, diff_kv * 2 * LANES
    assert cache_diff_k.shape[4] == 2 * LANES

    swa_kv, swa_hd = cache_swa_k.shape[3], cache_swa_k.shape[4]
    swa_heads = swa_sinks.shape[1]
    swa_group = swa_heads // swa_kv
    swa_nq, swa_nk = swa_heads * swa_hd, swa_kv * swa_hd
    assert cache_swa_k.shape[2] == WINDOW and s % WINDOW == 0 and LANES % swa_hd == 0

    mla_q_lora, mla_kv_lora = mla_w_uq.shape[1], mla_w_uk.shape[1]
    mla_heads, mla_nope = mla_w_uk.shape[2], mla_w_uk.shape[3]
    mla_rope, mla_vd = cache_mla_krope.shape[3], mla_w_uv.shape[3]
    assert mla_nope == LANES and LANES % mla_rope == 0 and mla_rope < LANES

    x = jnp.concatenate([x_prompt.reshape(bs, d), x_sample.reshape(db, d)], axis=0)
    pos = jnp.concatenate([jnp.tile(jnp.arange(s, dtype=jnp.int32), b),
                           jnp.full((db,), past_len, jnp.int32)])
    cos128, sin128 = _rope_tables(pos, LANES)
    cos64, sin64 = _rope_tables(pos, swa_hd)
    if mla_rope == swa_hd:
        cos_mr, sin_mr = cos64, sin64
    else:
        cos_mr, sin_mr = _rope_tables(pos, mla_rope)


    outs = {name: [] for name in ("dk_p", "dv_p", "dk_s", "dv_s", "wk_p", "wv_p", "wk_s", "wv_s",
                                  "ml_p", "mr_p", "ml_s", "mr_s")}
    for i in range(depth):
        kind, j = i % 3, i // 3
        x = _ffn(x, norm_ffn_pre[i], ffn_pre_gate, ffn_pre_up, ffn_pre_down, i)
        if kind == 0:
            lam_init = 0.8 - 0.6 * math.exp(-0.3 * i)
            qkv = _norm_mm(x, norm_mix[i], diff_w_qkv[j].astype(BF16), d)
            q, k, v = _diff_prep(qkv, cos128, sin128, diff_q_gain[j], diff_k_gain[j], diff_nq, diff_nk,
                                 LANES ** -0.5)
            lam_prm = (diff_lambda_q1[j], diff_lambda_k1[j], diff_lambda_q2[j], diff_lambda_k2[j])
            o_p = _diff_attn_prompt(q, k, v, lam_prm, diff_sub_gain[j], b, s, diff_kv, diff_group, lam_init)
            qs = q[bs:].reshape(db, diff_kv, diff_group, 2, 1, LANES)
            sel = jnp.eye(2, dtype=BF16).reshape(1, 1, 1, 2, 2, 1)
            qm = (qs * sel).reshape(db, diff_kv, 2 * diff_group, 2 * LANES)
            o_s = _diff_decode(qm, cache_diff_k, cache_diff_v, j, page_table, k[bs:], v[bs:], lam_prm,
                               diff_sub_gain[j], diff_group, lam_init)
            o = jnp.concatenate([o_p, o_s.reshape(db, diff_nq)], axis=0)
            x = _mm_res(o, diff_w_o[j].astype(BF16), x)
            outs["dk_p"].append(k[:bs].reshape(b, s, diff_kv, 2 * LANES))
            outs["dv_p"].append(v[:bs].reshape(b, s, diff_kv, 2 * LANES))
            outs["dk_s"].append(k[bs:].reshape(db, 1, diff_kv, 2 * LANES))
            outs["dv_s"].append(v[bs:].reshape(db, 1, diff_kv, 2 * LANES))
        elif kind == 1:
            qkv = _norm_mm(x, norm_mix[i], swa_w_qkv[j].astype(BF16), d)
            q, k, v = _swa_prep(qkv, cos64, sin64, swa_q_gain[j], swa_k_gain[j], swa_nq, swa_nk, swa_hd,
                                swa_hd ** -0.5)
            o_p = _swa_attn_prompt(q, k, v, swa_sinks[j], b, s, swa_kv, swa_group, swa_hd)
            qs = q[bs:].reshape(db, swa_kv, swa_group, 1, swa_hd)
            sel = jnp.eye(swa_kv, dtype=BF16).reshape(1, swa_kv, 1, swa_kv, 1)
            qbd = (qs * sel).reshape(db, swa_heads, swa_nk)
            o_s = _swa_decode(qbd, cache_swa_k, cache_swa_v, j, k[bs:], v[bs:], swa_sinks[j],
                              swa_kv, swa_group, swa_hd)
            o = jnp.concatenate([o_p, o_s.reshape(db, swa_nq)], axis=0)
            x = _mm_res(o, swa_w_o[j].astype(BF16), x)
            w_keep = min(WINDOW, s)
            k_p = k[:bs].reshape(b, s, swa_kv, swa_hd)
            v_p = v[:bs].reshape(b, s, swa_kv, swa_hd)
            outs["wk_p"].append(k_p[:, s - w_keep:])
            outs["wv_p"].append(v_p[:, s - w_keep:])
            outs["wk_s"].append(jnp.concatenate([cache_swa_k[j][:, 1:], k[bs:].reshape(db, 1, swa_kv, swa_hd)], axis=1))
            outs["wv_s"].append(jnp.concatenate([cache_swa_v[j][:, 1:], v[bs:].reshape(db, 1, swa_kv, swa_hd)], axis=1))
        else:
            down_n = mla_q_lora + mla_kv_lora + mla_rope
            w_down = jnp.pad(mla_w_down[j], ((0, 0), (0, LANES - mla_rope))).astype(BF16)
            dproj = _norm_mm(x, norm_mix[i], w_down, d)
            assert dproj.shape[1] == down_n + LANES - mla_rope
            lat, latb, kr_pad = _mla_down_prep(dproj, cos_mr, sin_mr, mla_kv_a_gain[j], mla_kr_gain[j],
                                               mla_q_lora, mla_kv_lora, mla_rope)
            w_uq = mla_w_uq[j].reshape(mla_q_lora, mla_heads, mla_nope + mla_rope)
            w_uq = jnp.pad(w_uq, ((0, 0), (0, 0), (0, 2 * LANES - mla_nope - mla_rope)))
            w_uq = w_uq.reshape(mla_q_lora, mla_heads * 2 * LANES).astype(BF16)
            qraw = _norm_mm(dproj, mla_q_a_gain[j], w_uq, mla_q_lora)
            qcat = _mla_q_prep(qraw, cos_mr, sin_mr, mla_qn_gain[j], mla_qr_gain[j], mla_heads, mla_rope,
                               (mla_nope + mla_rope) ** -0.5)
            w_uk = mla_w_uk[j].reshape(mla_kv_lora, mla_heads * mla_nope).astype(BF16)
            w_uv = mla_w_uv[j].reshape(mla_kv_lora, mla_heads * mla_vd).astype(BF16)
            kcat = _mla_k_prep(latb, kr_pad, w_uk, mla_kn_gain[j], mla_heads)
            o_p = _mla_attn_prompt(qcat, kcat, latb, w_uv, b, s, mla_heads)
            kr = kr_pad[:, :mla_rope]
            o_s = _mla_decode(qcat[bs:].reshape(db, mla_heads, 2 * LANES), cache_mla_latent, cache_mla_krope, j,
                              page_table, lat[bs:], kr[bs:], w_uk.T, mla_kn_gain[j], w_uv,
                              mla_heads, mla_nope, mla_rope)
            o = jnp.concatenate([o_p, o_s.reshape(db, mla_heads * mla_vd)], axis=0)
            x = _mm_res(o, mla_w_o[j].astype(BF16), x)
            outs["ml_p"].append(lat[:bs].reshape(b, s, mla_kv_lora))
            outs["mr_p"].append(kr[:bs].reshape(b, s, mla_rope))
            outs["ml_s"].append(lat[bs:].reshape(db, 1, mla_kv_lora))
            outs["mr_s"].append(kr[bs:].reshape(db, 1, mla_rope))
        x = _ffn(x, norm_ffn_post[i], ffn_post_gate, ffn_post_up, ffn_post_down, i)

    return (x[:bs].reshape(b, s, d), x[bs:].reshape(db, 1, d),
            *[jnp.stack(outs[name]) for name in ("dk_p", "dv_p", "dk_s", "dv_s", "wk_p", "wv_p", "wk_s", "wv_s",
                                                 "ml_p", "mr_p", "ml_s", "mr_s")])
```
